```python
import math
import jax
import jax.numpy as jnp
from jax import lax
import numpy as np

D_MODEL = 1024
BATCH = 4
SEQ = 4096
DEPTH = 4

GRID_W = 64
CTX_LEN = 256
N_MIXERS = 3
N_LAYERS_A = (DEPTH + 2) // 3
N_LAYERS_B = (DEPTH + 1) // 3
N_LAYERS_C = DEPTH // 3
DEEPNORM_ALPHA = (2 * DEPTH) ** 0.25
DEEPNORM_BETA = (8 * DEPTH) ** -0.25
LN_EPS = 1e-5
RMS_EPS = 1e-6
ROPE_THETA = 10000.0
Q_BLOCK = 128
MLA_HEADS = 8
MLA_Q_RANK = 768
MLA_KV_RANK = 256
MLA_NOPE = 128
MLA_ROPE = 64
MLA_V = 128
HY_ORDER = 2
HY_EMB = 33
HY_BANDS = (HY_EMB - 1) // 2
HY_FILTER_HIDDEN = 64
HY_SHORT = 3
HY_FAST_DECAY = 0.3
HY_SLOW_DECAY = 1.5
HY_TARGET = 1e-2
HY_SHIFT = 0.05
GQA_Q_HEADS = 8
GQA_KV_HEADS = 2
GQA_HEAD_DIM = 128
PEER_HEADS = 8
PEER_N_KEYS = 128
PEER_N_EXPERTS = PEER_N_KEYS ** 2
PEER_TOPK = 16
PEER_D_KEY = 256
PEER_CHUNK = 128

kernel_name = "hybrid_mla_hyena_gqa_peer_diffusion_block"


def layer_norm(x, g, b):
    xf = x.astype(jnp.float32)
    mu = xf.mean(-1, keepdims=True)
    var = jnp.square(xf - mu).mean(-1, keepdims=True)
    return ((xf - mu) * lax.rsqrt(var + LN_EPS) * g + b).astype(x.dtype)


def rms_norm(x, g):
    xf = x.astype(jnp.float32)
    return (xf * lax.rsqrt(jnp.square(xf).mean(-1, keepdims=True) + RMS_EPS) * g).astype(x.dtype)


def modulate(x, shift, scale):
    return x * (1.0 + scale) + shift


def axial_rope_tables(n_tokens, rot_dim):
    rows = n_tokens // GRID_W
    row = jnp.repeat(jnp.arange(rows, dtype=jnp.float32), GRID_W)
    col = jnp.tile(jnp.arange(GRID_W, dtype=jnp.float32), rows)
    quarter = rot_dim // 4
    inv_freq = ROPE_THETA ** (-jnp.arange(quarter, dtype=jnp.float32) / quarter)
    ang = jnp.concatenate([row[:, None] * inv_freq, col[:, None] * inv_freq], -1)
    return jnp.cos(ang), jnp.sin(ang)


def apply_rope(x, cos, sin):
    xf = x.astype(jnp.float32).reshape(x.shape[:-1] + (x.shape[-1] // 2, 2))
    c = cos[None, :, None, :]
    s = sin[None, :, None, :]
    x0, x1 = xf[..., 0], xf[..., 1]
    out = jnp.stack([x0 * c - x1 * s, x0 * s + x1 * c], -1)
    return out.reshape(x.shape).astype(x.dtype)


def attention(q, k, v, scale):
    b, lq, hk, g, dq = q.shape
    nb = lq // Q_BLOCK
    qb = jnp.moveaxis(q.reshape(b, nb, Q_BLOCK, hk, g, dq), 1, 0)

    def one_block(qi):
        s = jnp.einsum('bqhgd,bkhd->bhgqk', qi, k, preferred_element_type=jnp.float32) * scale
        p = jax.nn.softmax(s, axis=-1)
        return jnp.einsum('bhgqk,bkhe->bqhge', p.astype(v.dtype), v)

    o = lax.map(one_block, qb)
    return jnp.moveaxis(o, 0, 1).reshape(b, lq, hk * g * v.shape[-1])


def mla_mixer(h_c, h_l, cos, sin, w_dq, q_norm, w_uq, w_dkv, kv_norm, w_ukv, w_o, ctx_out):
    scale = (MLA_NOPE + MLA_ROPE) ** -0.5

    def queries(h, rope):
        b, l, _ = h.shape
        cq = rms_norm(h @ w_dq, q_norm)
        q = (cq @ w_uq).reshape(b, l, MLA_HEADS, MLA_NOPE + MLA_ROPE)
        if rope:
            q = jnp.concatenate([q[..., :MLA_NOPE], apply_rope(q[..., MLA_NOPE:], cos, sin)], -1)
        return q[:, :, :, None, :]

    def keys_values(h, rope):
        b, l, _ = h.shape
        ckv = h @ w_dkv
        c_kv = rms_norm(ckv[..., :MLA_KV_RANK], kv_norm)
        k_pe = ckv[..., MLA_KV_RANK:][:, :, None, :]
        if rope:
            k_pe = apply_rope(k_pe, cos, sin)
        kv = (c_kv @ w_ukv).reshape(b, l, MLA_HEADS, MLA_NOPE + MLA_V)
        k = jnp.concatenate([kv[..., :MLA_NOPE], jnp.broadcast_to(k_pe, (b, l, MLA_HEADS, MLA_ROPE))], -1)
        return k, kv[..., MLA_NOPE:]

    k_c, v_c = keys_values(h_c, False)
    k_l, v_l = keys_values(h_l, True)
    k_all = jnp.concatenate([k_c, k_l], 1)
    v_all = jnp.concatenate([v_c, v_l], 1)
    y_l = attention(queries(h_l, True), k_all, v_all, scale) @ w_o
    y_c = attention(queries(h_c, False), k_c, v_c, scale) @ w_o if ctx_out else None
    return y_c, y_l


def gqa_mixer(h_c, h_l, cos, sin, w_qkv, q_norm, k_norm, w_o, ctx_out):
    g = GQA_Q_HEADS // GQA_KV_HEADS
    nq = GQA_Q_HEADS * GQA_HEAD_DIM
    nk = GQA_KV_HEADS * GQA_HEAD_DIM
    scale = GQA_HEAD_DIM ** -0.5

    def project(h, rope):
        b, l, _ = h.shape
        qkv = h @ w_qkv
        q = rms_norm(qkv[..., :nq].reshape(b, l, GQA_Q_HEADS, GQA_HEAD_DIM), q_norm)
        k = rms_norm(qkv[..., nq:nq + nk].reshape(b, l, GQA_KV_HEADS, GQA_HEAD_DIM), k_norm)
        v = qkv[..., nq + nk:].reshape(b, l, GQA_KV_HEADS, GQA_HEAD_DIM)
        if rope:
            q = apply_rope(q, cos, sin)
            k = apply_rope(k, cos, sin)
        return q.reshape(b, l, GQA_KV_HEADS, g, GQA_HEAD_DIM), k, v

    q_c, k_c, v_c = project(h_c, False)
    q_l, k_l, v_l = project(h_l, True)
    k_all = jnp.concatenate([k_c, k_l], 1)
    v_all = jnp.concatenate([v_c, v_l], 1)
    y_l = attention(q_l, k_all, v_all, scale) @ w_o
    y_c = attention(q_c, k_c, v_c, scale) @ w_o if ctx_out else None
    return y_c, y_l


def hyena_filter_spectrum(length, w1, b1, fr1, w2, b2, fr2, w3):
    t01 = jnp.linspace(0.0, 1.0, length, dtype=jnp.float32)[:, None]
    w = 2.0 * math.pi * jnp.arange(length, dtype=jnp.float32)[:, None] / length
    f = jnp.linspace(1e-4, HY_BANDS - 1, HY_BANDS, dtype=jnp.float32)[None, :]
    emb = jnp.concatenate([t01, jnp.cos(f * w), -jnp.sin(f * w)], -1)
    hdn = jnp.sin(fr1 * (emb @ w1 + b1))
    hdn = jnp.sin(fr2 * (hdn @ w2 + b2))
    filt = (hdn @ w3).astype(jnp.float32).reshape(length, HY_ORDER, 2, D_MODEL)
    deltas = jnp.abs(jnp.linspace(math.log(HY_TARGET) / HY_SLOW_DECAY, math.log(HY_TARGET) / HY_FAST_DECAY,
                                  D_MODEL, dtype=jnp.float32))
    window = jnp.exp(-t01 * deltas[None, :]) + HY_SHIFT
    filt = filt * window[:, None, None, :]
    kernel = jnp.concatenate([filt[:, :, 0], jnp.zeros((1, HY_ORDER, D_MODEL), jnp.float32),
                              filt[:0:-1, :, 1]], 0)
    return jnp.fft.rfft(kernel, axis=0)


def long_conv(u, spec, bias):
    length = u.shape[1]
    uf = u.astype(jnp.float32)
    y = jnp.fft.irfft(jnp.fft.rfft(uf, n=2 * length, axis=1) * spec[None], n=2 * length, axis=1)[:, :length]
    return (y + uf * bias).astype(u.dtype)


def short_conv(z, w, b):
    length = z.shape[1]
    pad = HY_SHORT // 2
    zp = jnp.pad(z, ((0, 0), (pad, pad), (0, 0)))
    out = b
    for j in range(HY_SHORT):
        out = out + zp[:, j:j + length] * w[j]
    return out


def hyena_sequence(h, w_in, b_in, conv_w, conv_b, f_w1, f_b1, f_fr1, f_w2, f_b2, f_fr2, f_w3, bias, w_out, b_out):
    length = h.shape[1]
    z = short_conv(h @ w_in + b_in, conv_w, conv_b)
    x1, x2, v = jnp.split(z, 3, axis=-1)
    spec = hyena_filter_spectrum(length, f_w1, f_b1, f_fr1, f_w2, f_b2, f_fr2, f_w3)
    y = x1 * long_conv(v, spec[:, 0], bias[0])
    y = x2 * long_conv(y, spec[:, 1], bias[1])
    return y @ w_out + b_out


def peer_ffn(h, w_q, keys1, keys2, u, v):
    t, d = h.shape
    half = PEER_D_KEY // 2
    hb = h.reshape(t // PEER_CHUNK, PEER_CHUNK, d)

    def one_chunk(hc):
        q = (hc @ w_q).reshape(PEER_CHUNK, PEER_HEADS, 2, half)
        s1 = jnp.einsum('thd,hkd->thk', q[:, :, 0], keys1, preferred_element_type=jnp.float32)
        s2 = jnp.einsum('thd,hkd->thk', q[:, :, 1], keys2, preferred_element_type=jnp.float32)
        v1, i1 = lax.top_k(s1, PEER_TOPK)
        v2, i2 = lax.top_k(s2, PEER_TOPK)
        cand = (v1[..., :, None] + v2[..., None, :]).reshape(PEER_CHUNK, PEER_HEADS, PEER_TOPK * PEER_TOPK)
        sv, pos = lax.top_k(cand, PEER_TOPK)
        e1 = jnp.take_along_axis(i1, pos // PEER_TOPK, axis=-1)
        e2 = jnp.take_along_axis(i2, pos % PEER_TOPK, axis=-1)
        idx = e1 * PEER_N_KEYS + e2
        gate = jax.nn.softmax(sv, axis=-1)
        act = jax.nn.gelu(jnp.einsum('td,thkd->thk', hc, u[idx]), approximate=False)
        return jnp.einsum('thk,thkd->td', (gate * act).astype(hc.dtype), v[idx])

    return lax.map(one_chunk, hb).reshape(t, d)


def setup_inputs(seed: int = 0) -> dict:
    key = jax.random.key(seed)
    ks = iter(jax.random.split(key, 64))
    D = D_MODEL
    beta = DEEPNORM_BETA

    def nrm(shape, std):
        return std * jax.random.normal(next(ks), shape, jnp.float32)

    def gain(shape):
        return 1.0 + nrm(shape, 0.02)

    nA, nB, nC = N_LAYERS_A, N_LAYERS_B, N_LAYERS_C
    return {
        "x": nrm((BATCH, SEQ, D), 1.0),
        "c": nrm((BATCH, D), 1.0),
        "ctx": nrm((BATCH, CTX_LEN, D), 1.0),
        "c_ctx": nrm((D,), 1.0),
        "ada_w": nrm((DEPTH, D, 6 * D), D ** -0.5),
        "ada_b": nrm((DEPTH, 6 * D), 0.02),
        "ln_g": gain((DEPTH, 2, D)),
        "ln_b": nrm((DEPTH, 2, D), 0.02),
        "mla_w_dq": nrm((nA, D, MLA_Q_RANK), D ** -0.5),
        "mla_q_norm": gain((nA, MLA_Q_RANK)),
        "mla_w_uq": nrm((nA, MLA_Q_RANK, MLA_HEADS * (MLA_NOPE + MLA_ROPE)), MLA_Q_RANK ** -0.5),
        "mla_w_dkv": nrm((nA, D, MLA_KV_RANK + MLA_ROPE), D ** -0.5),
        "mla_kv_norm": gain((nA, MLA_KV_RANK)),
        "mla_w_ukv": nrm((nA, MLA_KV_RANK, MLA_HEADS * (MLA_NOPE + MLA_V)), MLA_KV_RANK ** -0.5),
        "mla_w_o": nrm((nA, MLA_HEADS * MLA_V, D), beta * (MLA_HEADS * MLA_V) ** -0.5),
        "hy_w_in": nrm((nB, D, 3 * D), D ** -0.5),
        "hy_b_in": nrm((nB, 3 * D), 0.02),
        "hy_conv_w": nrm((nB, HY_SHORT, 3 * D), HY_SHORT ** -0.5),
        "hy_conv_b": nrm((nB, 3 * D), 0.02),
        "hy_f_w1": nrm((nB, HY_EMB, HY_FILTER_HIDDEN), HY_EMB ** -0.5),
        "hy_f_b1": nrm((nB, HY_FILTER_HIDDEN), 0.02),
        "hy_f_fr1": gain((nB, HY_FILTER_HIDDEN)),
        "hy_f_w2": nrm((nB, HY_FILTER_HIDDEN, HY_FILTER_HIDDEN), HY_FILTER_HIDDEN ** -0.5),
        "hy_f_b2": nrm((nB, HY_FILTER_HIDDEN), 0.02),
        "hy_f_fr2": gain((nB, HY_FILTER_HIDDEN)),
        "hy_f_w3": nrm((nB, HY_FILTER_HIDDEN, HY_ORDER * 2 * D), 0.05 * HY_FILTER_HIDDEN ** -0.5),
        "hy_bias": nrm((nB, HY_ORDER, D), 0.1),
        "hy_w_out": nrm((nB, D, D), beta * D ** -0.5),
        "hy_b_out": nrm((nB, D), 0.02),
        "gqa_w_qkv": nrm((nC, D, (GQA_Q_HEADS + 2 * GQA_KV_HEADS) * GQA_HEAD_DIM), D ** -0.5),
        "gqa_q_norm": gain((nC, GQA_HEAD_DIM)),
        "gqa_k_norm": gain((nC, GQA_HEAD_DIM)),
        "gqa_w_o": nrm((nC, GQA_Q_HEADS * GQA_HEAD_DIM, D), beta * (GQA_Q_HEADS * GQA_HEAD_DIM) ** -0.5),
        "peer_w_q": nrm((DEPTH, D, PEER_HEADS * PEER_D_KEY), D ** -0.5),
        "peer_keys1": nrm((DEPTH, PEER_HEADS, PEER_N_KEYS, PEER_D_KEY // 2), (PEER_D_KEY // 2) ** -0.5),
        "peer_keys2": nrm((DEPTH, PEER_HEADS, PEER_N_KEYS, PEER_D_KEY // 2), (PEER_D_KEY // 2) ** -0.5),
        "peer_u": nrm((DEPTH, PEER_N_EXPERTS, D), D ** -0.5),
        "peer_v": nrm((DEPTH, PEER_N_EXPERTS, D), beta),
    }


def reference(x, c, ctx, c_ctx, ada_w, ada_b, ln_g, ln_b,
              mla_w_dq, mla_q_norm, mla_w_uq, mla_w_dkv, mla_kv_norm, mla_w_ukv, mla_w_o,
              hy_w_in, hy_b_in, hy_conv_w, hy_conv_b, hy_f_w1, hy_f_b1, hy_f_fr1, hy_f_w2, hy_f_b2, hy_f_fr2,
              hy_f_w3, hy_bias, hy_w_out, hy_b_out,
              gqa_w_qkv, gqa_q_norm, gqa_k_norm, gqa_w_o,
              peer_w_q, peer_keys1, peer_keys2, peer_u, peer_v):
    d = x.shape[-1]
    seq = x.shape[1]
    cos_a, sin_a = axial_rope_tables(seq, MLA_ROPE)
    cos_c, sin_c = axial_rope_tables(seq, GQA_HEAD_DIM)
    x_l, x_c = x, ctx
    ia = ib = ic = 0
    for layer in range(DEPTH):
        last = layer == DEPTH - 1
        mod_l = (jax.nn.silu(c) @ ada_w[layer] + ada_b[layer])[:, None, :]
        mod_c = (jax.nn.silu(c_ctx) @ ada_w[layer] + ada_b[layer])[None, None, :]
        sh_l, sc_l, g_l, shf_l, scf_l, gf_l = jnp.split(mod_l, 6, axis=-1)
        sh_c, sc_c, g_c, shf_c, scf_c, gf_c = jnp.split(mod_c, 6, axis=-1)
        h_l = modulate(x_l, sh_l, sc_l)
        h_c = modulate(x_c, sh_c, sc_c)
        kind = layer % N_MIXERS
        if kind == 0:
            y_c, y_l = mla_mixer(h_c, h_l, cos_a, sin_a, mla_w_dq[ia], mla_q_norm[ia], mla_w_uq[ia],
                                 mla_w_dkv[ia], mla_kv_norm[ia], mla_w_ukv[ia], mla_w_o[ia], not last)
            ia += 1
        elif kind == 1:
            hp = (hy_w_in[ib], hy_b_in[ib], hy_conv_w[ib], hy_conv_b[ib], hy_f_w1[ib], hy_f_b1[ib], hy_f_fr1[ib],
                  hy_f_w2[ib], hy_f_b2[ib], hy_f_fr2[ib], hy_f_w3[ib], hy_bias[ib], hy_w_out[ib], hy_b_out[ib])
            y_l = hyena_sequence(h_l, *hp)
            y_c = hyena_sequence(h_c, *hp) if not last else None
            ib += 1
        else:
            y_c, y_l = gqa_mixer(h_c, h_l, cos_c, sin_c, gqa_w_qkv[ic], gqa_q_norm[ic], gqa_k_norm[ic],
                                 gqa_w_o[ic], not last)
            ic += 1
        x_l = layer_norm(DEEPNORM_ALPHA * x_l + g_l * y_l, ln_g[layer, 0], ln_b[layer, 0])
        h_l = modulate(x_l, shf_l, scf_l)
        if last:
            f_l = peer_ffn(h_l.reshape(-1, d), peer_w_q[layer], peer_keys1[layer], peer_keys2[layer],
                           peer_u[layer], peer_v[layer]).reshape(x_l.shape)
        else:
            x_c = layer_norm(DEEPNORM_ALPHA * x_c + g_c * y_c, ln_g[layer, 0], ln_b[layer, 0])
            h_c = modulate(x_c, shf_c, scf_c)
            n_c = h_c.shape[0] * h_c.shape[1]
            tokens = jnp.concatenate([h_c.reshape(-1, d), h_l.reshape(-1, d)], 0)
            f_all = peer_ffn(tokens, peer_w_q[layer], peer_keys1[layer], peer_keys2[layer],
                             peer_u[layer], peer_v[layer])
            f_c = f_all[:n_c].reshape(x_c.shape)
            f_l = f_all[n_c:].reshape(x_l.shape)
            x_c = layer_norm(DEEPNORM_ALPHA * x_c + gf_c * f_c, ln_g[layer, 1], ln_b[layer, 1])
        x_l = layer_norm(DEEPNORM_ALPHA * x_l + gf_l * f_l, ln_g[layer, 1], ln_b[layer, 1])
    return x_l
```

```python
import functools
import math

import numpy as np
import jax
import jax.numpy as jnp
from jax import lax
from jax.experimental import pallas as pl
from jax.experimental.pallas import tpu as pltpu

F32 = jnp.float32
BF16 = jnp.bfloat16

DEPTH = 4
GRID_W = 64
N_MIXERS = 3
DEEPNORM_ALPHA = (2 * DEPTH) ** 0.25
LN_EPS = 1e-5
RMS_EPS = 1e-6
ROPE_THETA = 10000.0
MLA_HEADS = 8
MLA_Q_RANK = 768
MLA_KV_RANK = 256
MLA_NOPE = 128
MLA_ROPE = 64
MLA_V = 128
HY_ORDER = 2
HY_EMB = 33
HY_BANDS = (HY_EMB - 1) // 2
HY_SHORT = 3
HY_FAST_DECAY = 0.3
HY_SLOW_DECAY = 1.5
HY_TARGET = 1e-2
HY_SHIFT = 0.05
GQA_Q_HEADS = 8
GQA_KV_HEADS = 2
GQA_HEAD_DIM = 128
PEER_HEADS = 8
PEER_N_KEYS = 128
PEER_TOPK = 16
PEER_D_KEY = 256

LANES = 128
SUBLANES = 8
VMEM_LIMIT = 56 * 1024 * 1024


def _params(sem):
    return pltpu.CompilerParams(dimension_semantics=sem, vmem_limit_bytes=VMEM_LIMIT)


class Cfg:
    def __init__(self, batch, seq, ctx_len, d):
        self.B, self.L, self.CL, self.D = batch, seq, ctx_len, d
        self.NL = batch * seq
        self.NC = batch * ctx_len
        self.NT = self.NL + self.NC
        self.tm = math.gcd(512, math.gcd(seq, self.NC))


def _linear_kernel(*refs, has_mod, mod_rows, n_row, n_col, n_out, pre, epilogue):
    it = iter(refs)
    x_ref = next(it)
    mod_ref = next(it) if has_mod else None
    w_ref = next(it)
    row_refs = [next(it) for _ in range(n_row)]
    col_refs = [next(it) for _ in range(n_col)]
    out_refs = [next(it) for _ in range(n_out)]
    x = x_ref[...]
    mod = mod_ref[...] if has_mod else None
    if pre is not None:
        x = pre(x)
    if mod_rows is not None:
        sh, sc = mod_rows
        x = x * (1.0 + mod[sc:sc + 1, :]) + mod[sh:sh + 1, :]
    acc = jnp.dot(x.astype(BF16), w_ref[...].astype(BF16), preferred_element_type=F32)
    outs = epilogue(acc, [r[...] for r in row_refs], [c[...] for c in col_refs], mod)
    for o_ref, o in zip(out_refs, outs):
        o_ref[...] = o.astype(o_ref.dtype)


def fused_linear(x, w, *, tm, tn, out_defs, epilogue, seg_fn=None, mod=None, mod_rows=None,
                 w_layer=None, row_extras=(), col_extras=(), pre=None, n_rows=None):
    rows, k = x.shape
    if n_rows is not None:
        rows = n_rows
    n = w.shape[-1]
    grid = (rows // tm, n // tn)
    in_specs = [pl.BlockSpec((tm, k), lambda i, j: (i, 0))]
    args = [x]
    if mod is not None:
        in_specs.append(pl.BlockSpec((None,) + mod.shape[1:], lambda i, j: (seg_fn(i), 0, 0)))
        args.append(mod)
    if w_layer is None:
        in_specs.append(pl.BlockSpec((k, tn), lambda i, j: (0, j)))
    else:
        in_specs.append(pl.BlockSpec((None, k, tn), lambda i, j: (w_layer, 0, j)))
    args.append(w)
    for r in row_extras:
        in_specs.append(pl.BlockSpec((tm, r.shape[1]), lambda i, j: (i, 0)))
        args.append(r)
    for c in col_extras:
        in_specs.append(pl.BlockSpec((c.shape[0], tn), lambda i, j: (0, j)))
        args.append(c)
    out_specs = [pl.BlockSpec((tm, bc), lambda i, j: (i, j)) for (_, bc, _) in out_defs]
    out_shape = [jax.ShapeDtypeStruct((rows, tc), dt) for (tc, _, dt) in out_defs]
    kern = functools.partial(_linear_kernel, has_mod=mod is not None, mod_rows=mod_rows,
                             n_row=len(row_extras), n_col=len(col_extras), n_out=len(out_defs),
                             pre=pre, epilogue=epilogue)
    outs = pl.pallas_call(kern, grid=grid, in_specs=in_specs, out_specs=out_specs, out_shape=out_shape,
                          compiler_params=_params(("parallel", "parallel")))(*args)
    return outs


def _rms(y, gain):
    return y * lax.rsqrt(jnp.mean(jnp.square(y), axis=-1, keepdims=True) + RMS_EPS) * gain


def _rope(y, cos, sin):
    return y * cos + pltpu.roll(y, LANES // 2, 1) * sin


def _layer_norm(v, g, b):
    mu = jnp.mean(v, axis=-1, keepdims=True)
    vc = v - mu
    var = jnp.mean(jnp.square(vc), axis=-1, keepdims=True)
    return vc * lax.rsqrt(var + LN_EPS) * g + b


def residual_ln_linear(cfg, x, y, w, mod, gate_row, lng, lnb, bias=None, n_rows=None):
    d = cfg.D

    def epi(acc, rows, cols, m):
        if bias is not None:
            acc = acc + cols[2]
        v = DEEPNORM_ALPHA * rows[0] + m[gate_row:gate_row + 1, :] * acc
        return [_layer_norm(v, cols[0], cols[1])]

    cols = [lng.reshape(1, d), lnb.reshape(1, d)] + ([bias.reshape(1, d)] if bias is not None else [])
    return fused_linear(y, w, tm=cfg.tm, tn=d, out_defs=[(d, d, F32)], epilogue=epi,
                        seg_fn=_seg_fn(cfg), mod=mod, row_extras=[x], col_extras=cols, n_rows=n_rows)[0]


def _seg_fn(cfg):
    tiles_per_batch = cfg.L // cfg.tm
    nb = cfg.B
    return lambda i: jnp.minimum(i // tiles_per_batch, nb)


def _attn_kernel(*refs, n_parts, n_segs):
    q_refs = refs[:n_parts]
    pos = n_parts
    segs = []
    for _ in range(n_segs):
        segs.append((refs[pos:pos + n_parts], refs[pos + n_parts]))
        pos += n_parts + 1
    o_ref = refs[pos]
    scores = []
    for k_refs, _ in segs:
        sc = None
        for q_ref, k_ref in zip(q_refs, k_refs):
            t = lax.dot_general(q_ref[...], k_ref[...], (((1,), (1,)), ((), ())),
                                preferred_element_type=F32)
            sc = t if sc is None else sc + t
        scores.append(sc)
    m = None
    for sc in scores:
        ms = jnp.max(sc, axis=-1, keepdims=True)
        m = ms if m is None else jnp.maximum(m, ms)
    o = None
    den = None
    for sc, (_, v_ref) in zip(scores, segs):
        p = jnp.exp(sc - m)
        ds = jnp.sum(p, axis=-1, keepdims=True)
        den = ds if den is None else den + ds
        t = jnp.dot(p.astype(BF16), v_ref[...], preferred_element_type=F32)
        o = t if o is None else o + t
    o_ref[...] = (o / den).astype(o_ref.dtype)


def attention(q_arr, q_cols, q_rows, out_rows, key_segs, *, n_batch, n_heads, n_qt, tq, out_rows_total):
    n_parts = len(q_cols)
    in_specs = []
    args = []
    for qc in q_cols:
        in_specs.append(pl.BlockSpec((tq, LANES), lambda b, h, t, qc=qc: (q_rows(b, t), qc(h))))
        args.append(q_arr)
    for rows_blk, row_fn, k_parts, (v_arr, v_col) in key_segs:
        for k_arr, k_col in k_parts:
            in_specs.append(pl.BlockSpec((rows_blk, LANES),
                                         lambda b, h, t, row_fn=row_fn, k_col=k_col: (row_fn(b), k_col(h))))
            args.append(k_arr)
        in_specs.append(pl.BlockSpec((rows_blk, LANES),
                                     lambda b, h, t, row_fn=row_fn, v_col=v_col: (row_fn(b), v_col(h))))
        args.append(v_arr)
    out_spec = pl.BlockSpec((tq, LANES), lambda b, h, t: (out_rows(b, t), h))
    kern = functools.partial(_attn_kernel, n_parts=n_parts, n_segs=len(key_segs))
    return pl.pallas_call(
        kern, grid=(n_batch, n_heads, n_qt), in_specs=in_specs, out_specs=out_spec,
        out_shape=jax.ShapeDtypeStruct((out_rows_total, n_heads * LANES), BF16),
        compiler_params=_params(("parallel", "parallel", "arbitrary")))(*args)


def _attend_both(cfg, q_arr, q_cols, k_parts, v_part, n_heads, ctx_out):
    tq = min(256, cfg.L)
    n_qt = cfg.L // tq
    lat_blocks_ctx = cfg.NL // cfg.CL
    seg_lat = (cfg.L, lambda b: b, k_parts, v_part)
    seg_ctx = (cfg.CL, lambda b: lat_blocks_ctx + b, k_parts, v_part)
    lat_rows = lambda b, t: b * n_qt + t
    y_lat = attention(q_arr, q_cols, lat_rows, lat_rows, [seg_ctx, seg_lat],
                      n_batch=cfg.B, n_heads=n_heads, n_qt=n_qt, tq=tq, out_rows_total=cfg.NL)
    if not ctx_out:
        return y_lat
    y_ctx = attention(q_arr, q_cols, lambda b, t: lat_blocks_ctx + b, lambda b, t: b, [seg_ctx],
                      n_batch=cfg.B, n_heads=n_heads, n_qt=1, tq=cfg.CL, out_rows_total=cfg.NC)
    return jnp.concatenate([y_lat, y_ctx], axis=0)


def _rope_tables(cfg, rot_dim, half_width):
    rows = cfg.L // GRID_W
    row = jnp.repeat(jnp.arange(rows, dtype=F32), GRID_W)
    col = jnp.tile(jnp.arange(GRID_W, dtype=F32), rows)
    quarter = rot_dim // 4
    inv_freq = ROPE_THETA ** (-jnp.arange(quarter, dtype=F32) / quarter)
    ang = jnp.concatenate([row[:, None] * inv_freq, col[:, None] * inv_freq], -1)
    c, s = jnp.cos(ang), jnp.sin(ang)
    pad = half_width - rot_dim // 2
    one = jnp.ones((cfg.L, pad), F32)
    zero = jnp.zeros((cfg.L, pad), F32)
    cos_t = jnp.concatenate([c, one, c, one], -1)
    sin_t = jnp.concatenate([-s, zero, s, zero], -1)
    cos_t = jnp.concatenate([jnp.tile(cos_t, (cfg.B, 1)), jnp.ones((cfg.NC, LANES), F32)], 0)
    sin_t = jnp.concatenate([jnp.tile(sin_t, (cfg.B, 1)), jnp.zeros((cfg.NC, LANES), F32)], 0)
    return cos_t, sin_t


def _deinterleave_cols(w, rot_dim):
    half = rot_dim // 2
    pad = LANES // 2 - half
    ev = w[..., 0::2]
    od = w[..., 1::2]
    z = jnp.zeros(w.shape[:-1] + (pad,), w.dtype)
    return jnp.concatenate([ev, z, od, z], -1)


def mla_layer(cfg, xa, mod, w_dq, q_norm, w_uq, w_dkv, kv_norm, w_ukv, w_o, lng, lnb, ctx_out):
    d = cfg.D
    tm = cfg.tm
    seg = _seg_fn(cfg)
    scale = (MLA_NOPE + MLA_ROPE) ** -0.5
    cos_t, sin_t = _rope_tables(cfg, MLA_ROPE, LANES // 2)
    hq = MLA_HEADS

    w_uq3 = w_uq.reshape(MLA_Q_RANK, hq, MLA_NOPE + MLA_ROPE)
    w_uq_p = jnp.concatenate([w_uq3[..., :MLA_NOPE], _deinterleave_cols(w_uq3[..., MLA_NOPE:], MLA_ROPE)], -1)
    w_uq_p = w_uq_p.reshape(MLA_Q_RANK, hq * 2 * LANES).astype(BF16)
    w_dkv_p = jnp.concatenate([w_dkv[:, :MLA_KV_RANK], _deinterleave_cols(w_dkv[:, MLA_KV_RANK:], MLA_ROPE)],
                              -1).astype(BF16)

    def epi_cq(acc, rows, cols, m):
        return [_rms(acc, cols[0])]

    cq = fused_linear(xa, w_dq.astype(BF16), tm=tm, tn=MLA_Q_RANK, out_defs=[(MLA_Q_RANK, MLA_Q_RANK, BF16)],
                      epilogue=epi_cq, seg_fn=seg, mod=mod, mod_rows=(0, 1),
                      col_extras=[q_norm.reshape(1, -1)])[0]

    def epi_q(acc, rows, cols, m):
        nope = acc[:, :LANES] * scale
        pe = _rope(acc[:, LANES:] * scale, rows[0], rows[1])
        return [jnp.concatenate([nope, pe], -1)]

    q = fused_linear(cq, w_uq_p, tm=tm, tn=2 * LANES, out_defs=[(hq * 2 * LANES, 2 * LANES, BF16)],
                     epilogue=epi_q, row_extras=[cos_t, sin_t])[0]

    def epi_ckv(acc, rows, cols, m):
        return [_rms(acc[:, :MLA_KV_RANK], cols[0][:, :MLA_KV_RANK]),
                _rope(acc[:, MLA_KV_RANK:], rows[0], rows[1])]

    gain_pad = jnp.concatenate([kv_norm, jnp.ones((LANES,), F32)]).reshape(1, -1)
    ckv, kpe = fused_linear(xa, w_dkv_p, tm=tm, tn=MLA_KV_RANK + LANES,
                            out_defs=[(MLA_KV_RANK, MLA_KV_RANK, BF16), (LANES, LANES, BF16)],
                            epilogue=epi_ckv, seg_fn=seg, mod=mod, mod_rows=(0, 1),
                            row_extras=[cos_t, sin_t], col_extras=[gain_pad])
    kv = fused_linear(ckv, w_ukv.astype(BF16), tm=tm, tn=512, out_defs=[(w_ukv.shape[1], 512, BF16)],
                      epilogue=lambda acc, rows, cols, m: [acc])[0]

    y = _attend_both(cfg, q, [lambda h: 2 * h, lambda h: 2 * h + 1],
                     [(kv, lambda h: 2 * h), (kpe, lambda h: 0)], (kv, lambda h: 2 * h + 1), hq, ctx_out)
    n_rows = None if ctx_out else cfg.NL
    return residual_ln_linear(cfg, xa, y, w_o.astype(BF16), mod, 2, lng, lnb, n_rows=n_rows)


def gqa_layer(cfg, xa, mod, w_qkv, q_norm, k_norm, w_o, lng, lnb, ctx_out):
    tm = cfg.tm
    seg = _seg_fn(cfg)
    hd = GQA_HEAD_DIM
    nq, nk = GQA_Q_HEADS, GQA_KV_HEADS
    scale = hd ** -0.5
    cos_t, sin_t = _rope_tables(cfg, hd, LANES // 2)
    perm = np.concatenate([np.arange(0, hd, 2), np.arange(1, hd, 2)])
    w3 = w_qkv.reshape(cfg.D, nq + 2 * nk, hd)
    w_p = jnp.concatenate([w3[:, :nq + nk, perm], w3[:, nq + nk:, :]], 1).reshape(cfg.D, -1).astype(BF16)
    gains = jnp.stack([q_norm[perm], k_norm[perm]])
    n_chunks = nq + 2 * nk

    def epi(acc, rows, cols, m):
        g = cols[0]
        outs = []
        for c in range(n_chunks):
            y = acc[:, c * hd:(c + 1) * hd]
            if c < nq:
                y = _rope(_rms(y, g[0:1, :hd]) * scale, rows[0], rows[1])
            elif c < nq + nk:
                y = _rope(_rms(y, g[1:2, :hd]), rows[0], rows[1])
            outs.append(y)
        return [jnp.concatenate(outs, -1)]

    n_out = n_chunks * hd
    gains_full = jnp.tile(gains, (1, n_chunks))
    qkv = fused_linear(xa, w_p, tm=tm, tn=n_out, out_defs=[(n_out, n_out, BF16)], epilogue=epi,
                       seg_fn=seg, mod=mod, mod_rows=(0, 1), row_extras=[cos_t, sin_t],
                       col_extras=[gains_full])[0]
    grp = nq // nk
    y = _attend_both(cfg, qkv, [lambda h: h], [(qkv, lambda h: nq + h // grp)],
                     (qkv, lambda h: nq + nk + h // grp), nq, ctx_out)
    n_rows = None if ctx_out else cfg.NL
    return residual_ln_linear(cfg, xa, y, w_o.astype(BF16), mod, 2, lng, lnb, n_rows=n_rows)


PEER_E1_PER_STEP = 8
_PAIRS = [(a, b) for a in range(PEER_TOPK) for b in range(PEER_TOPK) if (a + 1) * (b + 1) <= PEER_TOPK]


def _insert_sorted(lst, x):
    out = []
    for a in lst:
        out.append(jnp.maximum(a, x))
        x = jnp.minimum(a, x)
    return out


def _gelu(x):
    return 0.5 * x * (1.0 + lax.erf(x * (2.0 ** -0.5)))


def _peer_route_group(s1_scr, s2_scr, cnt_scr, r2_scr, g):
    nk = PEER_N_KEYS
    k = PEER_TOPK
    col = pl.ds(g * LANES, LANES)
    neg = jnp.full((PEER_HEADS, LANES), -jnp.inf, F32)

    def key_rows(i):
        return pl.ds(pl.multiple_of(i * PEER_HEADS, PEER_HEADS), PEER_HEADS)

    def top_values(load):
        return lax.fori_loop(0, nk, lambda i, lst: tuple(_insert_sorted(lst, load(i))), (neg,) * k, unroll=4)

    v1 = top_values(lambda i: s1_scr[key_rows(i), col])
    v2 = top_values(lambda i: s2_scr[g, key_rows(i), :])
    cand = {ab: v1[ab[0]] + v2[ab[1]] for ab in _PAIRS}
    top = [neg] * k
    for ab in _PAIRS:
        top = _insert_sorted(top, cand[ab])
    tau = top[k - 1]
    cmax = cand[(0, 0)]
    zsum = jnp.zeros_like(tau)
    for ab in _PAIRS:
        zsum = zsum + jnp.where(cand[ab] >= tau, jnp.exp(cand[ab] - cmax), 0.0)
    inv_z = 1.0 / zsum
    m1, m2 = v1[0], v2[0]

    def finish(i, carry):
        rows = key_rows(i)
        x1 = s1_scr[rows, col]
        x2 = s2_scr[g, rows, :]
        cnt = jnp.zeros_like(x1)
        r2 = jnp.zeros_like(x2)
        for b in range(k):
            cnt = cnt + jnp.where(x1 + v2[b] >= tau, 1.0, 0.0)
            r2 = r2 + jnp.where(v2[b] > x2, 1.0, 0.0)
        cnt_scr[rows, col] = cnt
        r2_scr[g, rows, :] = r2
        s1_scr[rows, col] = jnp.exp(x1 - m1) * inv_z
        s2_scr[g, rows, :] = jnp.exp(x2 - m2)
        return carry

    lax.fori_loop(0, nk, finish, 0, unroll=2)


def _peer_kernel(x_ref, mod_ref, wq_ref, k1_ref, k2_ref, u_ref, vt_ref, lng_ref, lnb_ref, o_ref,
                 ht_scr, s1_scr, s2_scr, cnt_scr, r2_scr, r2b_scr, p2b_scr, act_scr, w_scr, acc_scr, *, tokens):
    e = pl.program_id(1)
    n_groups = tokens // LANES
    nh = PEER_HEADS
    nk = PEER_N_KEYS
    pack = 2 * SUBLANES

    @pl.when(e == 0)
    def _route():
        mod = mod_ref[...]
        h = x_ref[...] * (1.0 + mod[4:5, :]) + mod[3:4, :]
        ht = h.T.astype(BF16)
        ht_scr[...] = ht
        qt = jnp.dot(wq_ref[...], ht, preferred_element_type=F32)
        half_rows = nh * (PEER_D_KEY // 2)
        s1_scr[...] = jnp.dot(k1_ref[...], qt[:half_rows].astype(BF16), preferred_element_type=F32)
        s2 = jnp.dot(k2_ref[...], qt[half_rows:].astype(BF16), preferred_element_type=F32)
        for g in range(n_groups):
            s2_scr[g] = s2[:, g * LANES:(g + 1) * LANES]
        for g in range(n_groups):
            _peer_route_group(s1_scr, s2_scr, cnt_scr, r2_scr, g)
        for g in range(n_groups):
            col = pl.ds(g * LANES, LANES)
            for hd in range(nh):
                r2b_scr[hd, :, col] = r2_scr[g, pl.ds(hd, nk, stride=nh), :].astype(BF16)
                p2b_scr[hd, :, col] = s2_scr[g, pl.ds(hd, nk, stride=nh), :].astype(BF16)
        acc_scr[...] = jnp.zeros_like(acc_scr)

    act_scr[...] = jnp.dot(u_ref[...], ht_scr[...], preferred_element_type=F32)
    for j in range(PEER_E1_PER_STEP):
        row0 = pl.multiple_of((e * PEER_E1_PER_STEP + j) * nh, nh)
        for g in range(n_groups):
            col = pl.ds(g * LANES, LANES)
            cnt8 = cnt_scr[pl.ds(row0, nh), col]
            p18 = s1_scr[pl.ds(row0, nh), col]
            gate = jnp.zeros((nk // pack, pack, LANES), BF16)
            for hd in range(nh):
                cb = jnp.broadcast_to(cnt8[hd:hd + 1, :], (pack, LANES)).astype(BF16)
                pb = jnp.broadcast_to(p18[hd:hd + 1, :], (pack, LANES)).astype(BF16)
                r2t = r2b_scr[hd, :, col].reshape(nk // pack, pack, LANES)
                p2t = p2b_scr[hd, :, col].reshape(nk // pack, pack, LANES)
                gate = gate + jnp.where(r2t < cb[None], p2t, jnp.zeros_like(p2t)) * pb[None]
            a = _gelu(act_scr[pl.ds(j * nk, nk), col]).astype(BF16)
            w_scr[pl.ds(j * nk, nk), col] = gate.reshape(nk, LANES) * a
    acc_scr[...] += jnp.dot(vt_ref[...], w_scr[...], preferred_element_type=F32)

    @pl.when(e == pl.num_programs(1) - 1)
    def _finish():
        mod = mod_ref[...]
        f = acc_scr[...].T
        v = DEEPNORM_ALPHA * x_ref[...] + mod[5:6, :] * f
        o_ref[...] = _layer_norm(v, lng_ref[...], lnb_ref[...])


def peer_layer(cfg, xa, mod, layer, wq_t, k1_bd, k2_bd, u_b, vt_b, lng, lnb, n_rows):
    d = cfg.D
    t = cfg.tm
    n_exp = u_b.shape[1]
    e_tile = PEER_E1_PER_STEP * PEER_N_KEYS
    seg = _seg_fn(cfg)
    nq = wq_t.shape[1]
    nkh = k1_bd.shape[1]
    in_specs = [
        pl.BlockSpec((t, d), lambda i, e: (i, 0)),
        pl.BlockSpec((None, 6, d), lambda i, e: (seg(i), 0, 0)),
        pl.BlockSpec((None, nq, d), lambda i, e: (layer, 0, 0)),
        pl.BlockSpec((None, nkh, nkh), lambda i, e: (layer, 0, 0)),
        pl.BlockSpec((None, nkh, nkh), lambda i, e: (layer, 0, 0)),
        pl.BlockSpec((None, e_tile, d), lambda i, e: (layer, e, 0)),
        pl.BlockSpec((None, d, e_tile), lambda i, e: (layer, 0, e)),
        pl.BlockSpec((1, d), lambda i, e: (0, 0)),
        pl.BlockSpec((1, d), lambda i, e: (0, 0)),
    ]
    scratch = [
        pltpu.VMEM((d, t), BF16),
        pltpu.VMEM((nkh, t), F32),
        pltpu.VMEM((t // LANES, nkh, LANES), F32),
        pltpu.VMEM((nkh, t), F32),
        pltpu.VMEM((t // LANES, nkh, LANES), F32),
        pltpu.VMEM((PEER_HEADS, PEER_N_KEYS, t), BF16),
        pltpu.VMEM((PEER_HEADS, PEER_N_KEYS, t), BF16),
        pltpu.VMEM((e_tile, t), F32),
        pltpu.VMEM((e_tile, t), BF16),
        pltpu.VMEM((d, t), F32),
    ]
    return pl.pallas_call(
        functools.partial(_peer_kernel, tokens=t),
        grid=(n_rows // t, n_exp // e_tile), in_specs=in_specs,
        out_specs=pl.BlockSpec((t, d), lambda i, e: (i, 0)),
        out_shape=jax.ShapeDtypeStruct((n_rows, d), F32), scratch_shapes=scratch,
        compiler_params=_params(("parallel", "arbitrary")))(
            xa, mod, wq_t, k1_bd, k2_bd, u_b, vt_b, lng.reshape(1, d), lnb.reshape(1, d))


def peer_prepare(peer_w_q, peer_keys1, peer_keys2, peer_u, peer_v):
    depth, d, _ = peer_w_q.shape
    half = PEER_D_KEY // 2
    nh = PEER_HEADS
    wq_t = peer_w_q.reshape(depth, d, nh, 2, half).transpose(0, 3, 2, 4, 1).reshape(depth, 2 * nh * half, d)
    eye = jnp.eye(nh, dtype=F32)

    def bd(keys):
        return jnp.einsum('lhkd,hg->lkhgd', keys, eye).reshape(depth, PEER_N_KEYS * nh, nh * half).astype(BF16)

    return (wq_t.astype(BF16), bd(peer_keys1), bd(peer_keys2), peer_u.astype(BF16),
            peer_v.transpose(0, 2, 1).astype(BF16))


HY_C_TILE = LANES
ROW_PAD = SUBLANES


def _fft_sizes(length):
    n = 2 * length
    n1 = 64 if n >= 8192 else (32 if n >= 1024 else 16)
    return n1, n // n1


def _fft_mats(n1, n2):
    n = n1 * n2
    two_pi = 2.0 * math.pi
    k2 = jnp.arange(n2, dtype=jnp.int32)
    i1 = jnp.arange(n1, dtype=jnp.int32)
    ang = two_pi * ((k2[:, None] * k2[None, :]) % n2).astype(F32) / n2
    c1 = jnp.concatenate([jnp.cos(ang), -jnp.sin(ang)], 0)
    m = (i1[None, None, :] * k2[:, None, None] + i1[None, None, :] * i1[None, :, None] * n2) % n
    ang = two_pi * m.astype(F32) / n
    gr, gi = jnp.cos(ang), -jnp.sin(ang)
    g = jnp.concatenate([jnp.concatenate([gr, -gi], -1), jnp.concatenate([gi, gr], -1)], -2)
    hr, hi = jnp.swapaxes(gr, 1, 2), -jnp.swapaxes(gi, 1, 2)
    h = jnp.concatenate([jnp.concatenate([hr, -hi], -1), jnp.concatenate([hi, hr], -1)], -2)
    ang = two_pi * ((k2[:n2 // 2, None] * k2[None, :]) % n2).astype(F32) / n2
    c2 = jnp.concatenate([jnp.cos(ang), -jnp.sin(ang)], 1) / n
    return c1.astype(BF16), g.astype(BF16), h.astype(BF16), c2.astype(BF16)


def _fft_stage1(src, c1, a_scr, n1, n2, n2_in):
    pitch = 2 * n2 + ROW_PAD

    def body(i, carry):
        xs = src[pl.ds(i, n2_in, stride=n1), :].astype(BF16)
        a_scr[pl.ds(pl.multiple_of(i * pitch, SUBLANES), 2 * n2), :] = jnp.dot(c1, xs, preferred_element_type=F32)
        return carry

    lax.fori_loop(0, n1, body, 0)


def _fft_stage2_load(a_scr, k, n1, n2):
    pitch = 2 * n2 + ROW_PAD
    ar = a_scr[pl.ds(k, n1, stride=pitch), :]
    ai = a_scr[pl.ds(n2 + k, n1, stride=pitch), :]
    return jnp.concatenate([ar, ai], 0).astype(BF16)


def _short_conv(z, w, b):
    rows = z.shape[0]
    idx = lax.broadcasted_iota(jnp.int32, z.shape, 0)
    prev = jnp.where(idx == 0, 0.0, pltpu.roll(z, 1, 0))
    nxt = jnp.where(idx == rows - 1, 0.0, pltpu.roll(z, rows - 1, 0))
    return b + prev * w[0:1, :] + z * w[1:2, :] + nxt * w[2:3, :]


def _hyena_conv_kernel(*refs, first, n1, n2):
    if first:
        (zu_ref, cw_u_ref, cb_u_ref, zx_ref, cw_x_ref, cb_x_ref, bias_ref, sr_ref, si_ref,
         c1_ref, g_ref, h_ref, c2_ref, o_ref, u_scr, y_scr, a_scr, b_scr) = refs
        u_scr[...] = _short_conv(zu_ref[...], cw_u_ref[...], cb_u_ref[...])
    else:
        (zu_ref, zx_ref, cw_x_ref, cb_x_ref, bias_ref, sr_ref, si_ref,
         c1_ref, g_ref, h_ref, c2_ref, o_ref, u_scr, y_scr, a_scr, b_scr) = refs
        u_scr[...] = zu_ref[...]
    _fft_stage1(u_scr, c1_ref[...], a_scr, n1, n2, n2 // 2)
    pitch_b = 2 * n1 + ROW_PAD

    def mid(k, carry):
        x = jnp.dot(g_ref[k], _fft_stage2_load(a_scr, k, n1, n2), preferred_element_type=F32)
        rows = pl.ds(pl.multiple_of(k * n1, n1), n1)
        xr, xi = x[:n1], x[n1:]
        sr, si = sr_ref[rows, :], si_ref[rows, :]
        y = jnp.concatenate([xr * sr - xi * si, xr * si + xi * sr], 0).astype(BF16)
        b_scr[pl.ds(pl.multiple_of(k * pitch_b, SUBLANES), 2 * n1), :] = jnp.dot(
            h_ref[k], y, preferred_element_type=F32)
        return carry

    lax.fori_loop(0, n2, mid, 0)

    def last(i, carry):
        br = b_scr[pl.ds(i, n2, stride=pitch_b), :]
        bi = b_scr[pl.ds(n1 + i, n2, stride=pitch_b), :]
        y = jnp.dot(c2_ref[...], jnp.concatenate([br, bi], 0).astype(BF16), preferred_element_type=F32)
        y_scr[pl.ds(i, n2 // 2, stride=n1), :] = y
        return carry

    lax.fori_loop(0, n1, last, 0)
    x = _short_conv(zx_ref[...], cw_x_ref[...], cb_x_ref[...])
    u = u_scr[...]
    o_ref[...] = (x * (y_scr[...] + u * bias_ref[...])).astype(o_ref.dtype)


def _hyena_conv(y_prev, z, zu_col0, zx_col0, conv_w, conv_b, bias_row, spec_r, spec_i, spec_col0, mats,
                length, n_seq, row_blk0, out_dtype):
    n1, n2 = _fft_sizes(length)
    c1, g, h, c2 = mats
    c1 = c1[:, :n2 // 2]
    ct = HY_C_TILE
    d = bias_row.shape[1]
    n = 2 * length
    one = pl.Buffered(1)
    first = y_prev is None

    def z_specs(col0):
        return [pl.BlockSpec((length, ct), lambda c, s: (row_blk0 + s, col0 + c)),
                pl.BlockSpec((HY_SHORT, ct), lambda c, s: (0, col0 + c)),
                pl.BlockSpec((1, ct), lambda c, s: (0, col0 + c))]

    if first:
        in_specs = z_specs(zu_col0)
        args = [z, conv_w, conv_b]
    else:
        in_specs = [pl.BlockSpec((length, ct), lambda c, s: (s, c))]
        args = [y_prev]
    in_specs += z_specs(zx_col0) + [
        pl.BlockSpec((1, ct), lambda c, s: (0, c)),
        pl.BlockSpec((n, ct), lambda c, s: (0, spec_col0 + c), pipeline_mode=one),
        pl.BlockSpec((n, ct), lambda c, s: (0, spec_col0 + c), pipeline_mode=one),
        pl.BlockSpec(c1.shape, lambda c, s: (0, 0), pipeline_mode=one),
        pl.BlockSpec(g.shape, lambda c, s: (0, 0, 0), pipeline_mode=one),
        pl.BlockSpec(h.shape, lambda c, s: (0, 0, 0), pipeline_mode=one),
        pl.BlockSpec(c2.shape, lambda c, s: (0, 0), pipeline_mode=one),
    ]
    args += [z, conv_w, conv_b, bias_row, spec_r, spec_i, c1, g, h, c2]
    scratch = [
        pltpu.VMEM((length, ct), F32),
        pltpu.VMEM((length, ct), F32),
        pltpu.VMEM((n1 * (2 * n2 + ROW_PAD), ct), F32),
        pltpu.VMEM((n2 * (2 * n1 + ROW_PAD), ct), F32),
    ]
    return pl.pallas_call(
        functools.partial(_hyena_conv_kernel, first=first, n1=n1, n2=n2),
        grid=(d // ct, n_seq), in_specs=in_specs,
        out_specs=pl.BlockSpec((length, ct), lambda c, s: (s, c)),
        out_shape=jax.ShapeDtypeStruct((n_seq * length, d), out_dtype), scratch_shapes=scratch,
        compiler_params=_params(("parallel", "arbitrary")))(*args)


def _hyena_filter_kernel(emb_ref, w1_ref, b1_ref, fr1_ref, w2_ref, b2_ref, fr2_ref, w3_ref, dl_ref, o_ref):
    hi = lax.Precision.HIGHEST
    emb = emb_ref[...]
    hdn = jnp.sin(fr1_ref[...] * (jnp.dot(emb, w1_ref[...], precision=hi, preferred_element_type=F32) + b1_ref[...]))
    hdn = jnp.sin(fr2_ref[...] * (jnp.dot(hdn, w2_ref[...], precision=hi, preferred_element_type=F32) + b2_ref[...]))
    filt = jnp.dot(hdn, w3_ref[...], precision=hi, preferred_element_type=F32)
    t01 = emb[:, 0:1]
    mask = emb[:, LANES - 1:LANES]
    window = jnp.exp(-t01 * dl_ref[...]) + HY_SHIFT
    o_ref[...] = filt * window * mask


def _hyena_spec_kernel(k_ref, c1_ref, g_ref, sr_ref, si_ref, a_scr, *, n1, n2):
    _fft_stage1(k_ref, c1_ref[...], a_scr, n1, n2, n2)

    def mid(k, carry):
        x = jnp.dot(g_ref[k], _fft_stage2_load(a_scr, k, n1, n2), preferred_element_type=F32)
        rows = pl.ds(pl.multiple_of(k * n1, n1), n1)
        sr_ref[rows, :] = x[:n1]
        si_ref[rows, :] = x[n1:]
        return carry

    lax.fori_loop(0, n2, mid, 0)


def _hyena_spectra(length, d, f_w1, f_b1, f_fr1, f_w2, f_b2, f_fr2, f_w3, mats):
    n = 2 * length
    n1, n2 = _fft_sizes(length)
    hid = f_w1.shape[1]
    lag = jnp.concatenate([jnp.arange(length), jnp.zeros((1,), jnp.int32), jnp.arange(length - 1, 0, -1)])
    t01 = jnp.linspace(0.0, 1.0, length, dtype=F32)[:, None]
    w = 2.0 * math.pi * jnp.arange(length, dtype=F32)[:, None] / length
    f = jnp.linspace(1e-4, HY_BANDS - 1, HY_BANDS, dtype=F32)[None, :]
    emb = jnp.concatenate([t01, jnp.cos(f * w), -jnp.sin(f * w)], -1)[lag]
    mask = jnp.ones((n, 1), F32).at[length, 0].set(0.0)
    emb = jnp.concatenate([emb, jnp.zeros((n, LANES - HY_EMB - 1), F32), mask], -1)

    def pad2(a, r, c):
        return jnp.zeros((r, c), F32).at[:a.shape[0], :a.shape[1]].set(a)

    w1 = pad2(f_w1, LANES, LANES)
    b1 = pad2(f_b1[None], 1, LANES)
    fr1 = pad2(f_fr1[None], 1, LANES)
    w2 = pad2(f_w2, LANES, LANES)
    b2 = pad2(f_b2[None], 1, LANES)
    fr2 = pad2(f_fr2[None], 1, LANES)
    w3 = pad2(f_w3, LANES, f_w3.shape[1])
    deltas = jnp.abs(jnp.linspace(math.log(HY_TARGET) / HY_SLOW_DECAY, math.log(HY_TARGET) / HY_FAST_DECAY,
                                  d, dtype=F32))[None, :]
    tr = min(512, length)
    half_tiles = length // tr
    small = lambda shape: pl.BlockSpec(shape, lambda r, o: (0, 0))
    kern = pl.pallas_call(
        _hyena_filter_kernel, grid=(n // tr, HY_ORDER),
        in_specs=[pl.BlockSpec((tr, LANES), lambda r, o: (r, 0)),
                  small((LANES, LANES)), small((1, LANES)), small((1, LANES)),
                  small((LANES, LANES)), small((1, LANES)), small((1, LANES)),
                  pl.BlockSpec((LANES, d), lambda r, o: (0, 2 * o + r // half_tiles)),
                  small((1, d))],
        out_specs=pl.BlockSpec((tr, d), lambda r, o: (r, o)),
        out_shape=jax.ShapeDtypeStruct((n, HY_ORDER * d), F32),
        compiler_params=_params(("parallel", "parallel")))(emb, w1, b1, fr1, w2, b2, fr2, w3, deltas)
    c1, g, _, _ = mats
    ct = HY_C_TILE
    n_cols = HY_ORDER * d
    one = pl.Buffered(1)
    spec_r, spec_i = pl.pallas_call(
        functools.partial(_hyena_spec_kernel, n1=n1, n2=n2), grid=(n_cols // ct,),
        in_specs=[pl.BlockSpec((n, ct), lambda c: (0, c)),
                  pl.BlockSpec(c1.shape, lambda c: (0, 0), pipeline_mode=one),
                  pl.BlockSpec(g.shape, lambda c: (0, 0, 0), pipeline_mode=one)],
        out_specs=[pl.BlockSpec((n, ct), lambda c: (0, c)), pl.BlockSpec((n, ct), lambda c: (0, c))],
        out_shape=[jax.ShapeDtypeStruct((n, n_cols), F32)] * 2,
        scratch_shapes=[pltpu.VMEM((n1 * (2 * n2 + ROW_PAD), ct), F32)],
        compiler_params=_params(("parallel",)))(kern, c1, g)
    return spec_r, spec_i


def hyena_layer(cfg, xa, mod, w_in, b_in, conv_w, conv_b, f_w1, f_b1, f_fr1, f_w2, f_b2, f_fr2, f_w3,
                bias, w_out, b_out, lng, lnb, ctx_out):
    d = cfg.D
    n_ct = d // HY_C_TILE
    z = fused_linear(xa, w_in.astype(BF16), tm=cfg.tm, tn=512, out_defs=[(3 * d, 512, F32)],
                     epilogue=lambda acc, rows, cols, m: [acc + cols[0]], seg_fn=_seg_fn(cfg), mod=mod,
                     mod_rows=(0, 1), col_extras=[b_in.reshape(1, -1)],
                     n_rows=None if ctx_out else cfg.NL)[0]
    conv_b2 = conv_b.reshape(1, -1)
    seqs = [(cfg.L, cfg.B, 0)]
    if ctx_out:
        seqs.append((cfg.CL, cfg.B, cfg.NL // cfg.CL))
    ys = []
    for length, n_seq, row_blk0 in seqs:
        mats = _fft_mats(*_fft_sizes(length))
        spec_r, spec_i = _hyena_spectra(length, d, f_w1, f_b1, f_fr1, f_w2, f_b2, f_fr2, f_w3, mats)
        y1 = _hyena_conv(None, z, 2 * n_ct, 0, conv_w, conv_b2, bias[0:1], spec_r, spec_i, 0,
                         mats, length, n_seq, row_blk0, F32)
        ys.append(_hyena_conv(y1, z, 0, n_ct, conv_w, conv_b2, bias[1:2], spec_r, spec_i, n_ct,
                              mats, length, n_seq, row_blk0, BF16))
    y = ys[0] if len(ys) == 1 else jnp.concatenate(ys, 0)
    n_rows = None if ctx_out else cfg.NL
    return residual_ln_linear(cfg, xa, y, w_out.astype(BF16), mod, 2, lng, lnb, bias=b_out, n_rows=n_rows)


def ada_modulation(cfg, cvec, ada_w, ada_b, layer):
    d = cfg.D
    out = fused_linear(cvec, ada_w, w_layer=layer, tm=cvec.shape[0], tn=d, out_defs=[(6 * d, d, F32)],
                       epilogue=lambda acc, rows, cols, m: [acc + cols[0]], pre=jax.nn.silu,
                       col_extras=[ada_b[layer].reshape(1, -1)])[0]
    return out[:cfg.B + 1].reshape(cfg.B + 1, 6, d)


def kernel(x, c, ctx, c_ctx, ada_w, ada_b, ln_g, ln_b, mla_w_dq, mla_q_norm, mla_w_uq, mla_w_dkv, mla_kv_norm,
           mla_w_ukv, mla_w_o, hy_w_in, hy_b_in, hy_conv_w, hy_conv_b, hy_f_w1, hy_f_b1, hy_f_fr1, hy_f_w2,
           hy_f_b2, hy_f_fr2, hy_f_w3, hy_bias, hy_w_out, hy_b_out, gqa_w_qkv, gqa_q_norm, gqa_k_norm, gqa_w_o,
           peer_w_q, peer_keys1, peer_keys2, peer_u, peer_v):
    batch, seq, d = x.shape
    cfg = Cfg(batch, seq, ctx.shape[1], d)
    xa = jnp.concatenate([x.reshape(-1, d), ctx.reshape(-1, d)], 0)
    pad_rows = -(batch + 1) % SUBLANES
    cvec = jnp.concatenate([c, c_ctx[None, :], jnp.zeros((pad_rows, d), F32)], 0)
    peer_w = peer_prepare(peer_w_q, peer_keys1, peer_keys2, peer_u, peer_v)
    ia = ib = ic = 0
    for layer in range(DEPTH):
        last = layer == DEPTH - 1
        mod = ada_modulation(cfg, cvec, ada_w, ada_b, layer)
        lng, lnb = ln_g[layer, 0], ln_b[layer, 0]
        kind = layer % N_MIXERS
        if kind == 0:
            xm = mla_layer(cfg, xa, mod, mla_w_dq[ia], mla_q_norm[ia], mla_w_uq[ia], mla_w_dkv[ia],
                           mla_kv_norm[ia], mla_w_ukv[ia], mla_w_o[ia], lng, lnb, not last)
            ia += 1
        elif kind == 1:
            xm = hyena_layer(cfg, xa, mod, hy_w_in[ib], hy_b_in[ib], hy_conv_w[ib], hy_conv_b[ib], hy_f_w1[ib],
                             hy_f_b1[ib], hy_f_fr1[ib], hy_f_w2[ib], hy_f_b2[ib], hy_f_fr2[ib], hy_f_w3[ib],
                             hy_bias[ib], hy_w_out[ib], hy_b_out[ib], lng, lnb, not last)
            ib += 1
        else:
            xm = gqa_layer(cfg, xa, mod, gqa_w_qkv[ic], gqa_q_norm[ic], gqa_k_norm[ic], gqa_w_o[ic],
                           lng, lnb, not last)
            ic += 1
        xa = peer_layer(cfg, xm, mod, layer, *peer_w, ln_g[layer, 1], ln_b[layer, 1], xm.shape[0])
    return xa[:cfg.NL].reshape(batch, seq, d)
```

```python
import functools
import math

import numpy as np
import jax
import jax.numpy as jnp
from jax import lax
from jax.experimental import pallas as pl
from jax.experimental.pallas import tpu as pltpu

F32 = jnp.float32
BF16 = jnp.bfloat16

DEPTH = 4
GRID_W = 64
N_MIXERS = 3
DEEPNORM_ALPHA = (2 * DEPTH) ** 0.25
LN_EPS = 1e-5
RMS_EPS = 1e-6
ROPE_THETA = 10000.0
MLA_HEADS = 8
MLA_Q_RANK = 768
MLA_KV_RANK = 256
MLA_NOPE = 128
MLA_ROPE = 64
MLA_V = 128
HY_ORDER = 2
HY_EMB = 33
HY_BANDS = (HY_EMB - 1) // 2
HY_SHORT = 3
HY_FAST_DECAY = 0.3
HY_SLOW_DECAY = 1.5
HY_TARGET = 1e-2
HY_SHIFT = 0.05
GQA_Q_HEADS = 8
GQA_KV_HEADS = 2
GQA_HEAD_DIM = 128
PEER_HEADS = 8
PEER_N_KEYS = 128
PEER_TOPK = 16
PEER_D_KEY = 256

LANES = 128
SUBLANES = 8
VMEM_LIMIT = 56 * 1024 * 1024


def _params(sem):
    return pltpu.CompilerParams(dimension_semantics=sem, vmem_limit_bytes=VMEM_LIMIT)


class Cfg:
    def __init__(self, batch, seq, ctx_len, d):
        self.B, self.L, self.CL, self.D = batch, seq, ctx_len, d
        self.NL = batch * seq
        self.NC = batch * ctx_len
        self.NT = self.NL + self.NC
        self.tm = math.gcd(512, math.gcd(seq, self.NC))


def _linear_kernel(*refs, has_mod, mod_rows, n_row, n_col, n_out, pre, epilogue):
    it = iter(refs)
    x_ref = next(it)
    mod_ref = next(it) if has_mod else None
    w_ref = next(it)
    row_refs = [next(it) for _ in range(n_row)]
    col_refs = [next(it) for _ in range(n_col)]
    out_refs = [next(it) for _ in range(n_out)]
    x = x_ref[...]
    mod = mod_ref[...] if has_mod else None
    if pre is not None:
        x = pre(x)
    if mod_rows is not None:
        sh, sc = mod_rows
        x = x * (1.0 + mod[sc:sc + 1, :]) + mod[sh:sh + 1, :]
    acc = jnp.dot(x.astype(BF16), w_ref[...].astype(BF16), preferred_element_type=F32)
    outs = epilogue(acc, [r[...] for r in row_refs], [c[...] for c in col_refs], mod)
    for o_ref, o in zip(out_refs, outs):
        o_ref[...] = o.astype(o_ref.dtype)


def fused_linear(x, w, *, tm, tn, out_defs, epilogue, seg_fn=None, mod=None, mod_rows=None,
                 w_layer=None, row_extras=(), col_extras=(), pre=None, n_rows=None):
    rows, k = x.shape
    if n_rows is not None:
        rows = n_rows
    n = w.shape[-1]
    grid = (rows // tm, n // tn)
    in_specs = [pl.BlockSpec((tm, k), lambda i, j: (i, 0))]
    args = [x]
    if mod is not None:
        in_specs.append(pl.BlockSpec((None,) + mod.shape[1:], lambda i, j: (seg_fn(i), 0, 0)))
        args.append(mod)
    if w_layer is None:
        in_specs.append(pl.BlockSpec((k, tn), lambda i, j: (0, j)))
    else:
        in_specs.append(pl.BlockSpec((None, k, tn), lambda i, j: (w_layer, 0, j)))
    args.append(w)
    for r in row_extras:
        in_specs.append(pl.BlockSpec((tm, r.shape[1]), lambda i, j: (i, 0)))
        args.append(r)
    for c in col_extras:
        in_specs.append(pl.BlockSpec((c.shape[0], tn), lambda i, j: (0, j)))
        args.append(c)
    out_specs = [pl.BlockSpec((tm, bc), lambda i, j: (i, j)) for (_, bc, _) in out_defs]
    out_shape = [jax.ShapeDtypeStruct((rows, tc), dt) for (tc, _, dt) in out_defs]
    kern = functools.partial(_linear_kernel, has_mod=mod is not None, mod_rows=mod_rows,
                             n_row=len(row_extras), n_col=len(col_extras), n_out=len(out_defs),
                             pre=pre, epilogue=epilogue)
    outs = pl.pallas_call(kern, grid=grid, in_specs=in_specs, out_specs=out_specs, out_shape=out_shape,
                          compiler_params=_params(("parallel", "parallel")))(*args)
    return outs


def _rms(y, gain):
    return y * lax.rsqrt(jnp.mean(jnp.square(y), axis=-1, keepdims=True) + RMS_EPS) * gain


def _rope(y, cos, sin):
    return y * cos + pltpu.roll(y, LANES // 2, 1) * sin


def _layer_norm(v, g, b):
    mu = jnp.mean(v, axis=-1, keepdims=True)
    vc = v - mu
    var = jnp.mean(jnp.square(vc), axis=-1, keepdims=True)
    return vc * lax.rsqrt(var + LN_EPS) * g + b


def residual_ln_linear(cfg, x, y, w, mod, gate_row, lng, lnb, bias=None, n_rows=None):
    d = cfg.D

    def epi(acc, rows, cols, m):
        if bias is not None:
            acc = acc + cols[2]
        v = DEEPNORM_ALPHA * rows[0] + m[gate_row:gate_row + 1, :] * acc
        return [_layer_norm(v, cols[0], cols[1])]

    cols = [lng.reshape(1, d), lnb.reshape(1, d)] + ([bias.reshape(1, d)] if bias is not None else [])
    return fused_linear(y, w, tm=cfg.tm, tn=d, out_defs=[(d, d, F32)], epilogue=epi,
                        seg_fn=_seg_fn(cfg), mod=mod, row_extras=[x], col_extras=cols, n_rows=n_rows)[0]


def _seg_fn(cfg):
    tiles_per_batch = cfg.L // cfg.tm
    nb = cfg.B
    return lambda i: jnp.minimum(i // tiles_per_batch, nb)


ATTN_KEY_CHUNK = 256
ATTN_CHUNK_UNROLL = 8


def _attn_kernel(*refs, n_parts, n_segs, heads, tq):
    q_ref = refs[0]
    pos = 1
    segs = []
    for _ in range(n_segs):
        segs.append((refs[pos:pos + n_parts], refs[pos + n_parts]))
        pos += n_parts + 1
    o_ref, sc_scr, acc_scr = refs[pos], refs[pos + 1], refs[pos + 2]
    width = n_parts * LANES
    qv = q_ref[...]
    q = qv if heads == 1 else jnp.concatenate([qv[:, i * width:(i + 1) * width] for i in range(heads)], 0)
    m = None
    row0 = 0
    for k_refs, _ in segs:
        k = k_refs[0][...] if n_parts == 1 else jnp.concatenate([r[...] for r in k_refs], 1)
        sc = lax.dot_general(k, q, (((1,), (1,)), ((), ())), preferred_element_type=F32)
        sc_scr[pl.ds(row0, sc.shape[0]), :] = sc
        row0 += sc.shape[0]
        ms = jnp.max(sc, axis=0, keepdims=True)
        m = ms if m is None else jnp.maximum(m, ms)
    ones = jnp.ones((2 * SUBLANES, ATTN_KEY_CHUNK), BF16)
    acc_scr[...] = jnp.zeros_like(acc_scr)
    row0 = 0
    for _, vt_ref in segs:
        def chunk(c, carry, vt_ref=vt_ref, row0=row0):
            keys = pl.ds(pl.multiple_of(row0 + c * ATTN_KEY_CHUNK, ATTN_KEY_CHUNK), ATTN_KEY_CHUNK)
            cols = pl.ds(pl.multiple_of(c * ATTN_KEY_CHUNK, ATTN_KEY_CHUNK), ATTN_KEY_CHUNK)
            p = jnp.exp((sc_scr[keys, :] - m).astype(BF16))
            vt = jnp.concatenate([vt_ref[:, cols], ones], 0)
            acc_scr[...] += jnp.dot(vt, p, preferred_element_type=F32)
            return carry

        n_chunks = vt_ref.shape[1] // ATTN_KEY_CHUNK
        lax.fori_loop(0, n_chunks, chunk, 0, unroll=min(n_chunks, ATTN_CHUNK_UNROLL))
        row0 += vt_ref.shape[1]
    acc = acc_scr[...]
    o = (acc[:LANES] / acc[LANES:LANES + 1]).T
    if heads > 1:
        o = jnp.concatenate([o[i * tq:(i + 1) * tq] for i in range(heads)], 1)
    o_ref[...] = o.astype(o_ref.dtype)


def attention(q_arr, q_rows, out_rows, key_segs, *, n_parts, heads, n_batch, n_groups, n_qt, tq, out_rows_total):
    qw = heads * n_parts * LANES
    in_specs = [pl.BlockSpec((tq, qw), lambda b, g, t: (q_rows(b, t), g))]
    args = [q_arr]
    for keys_blk, blk_fn, k_parts, (vt_arr, vt_row) in key_segs:
        for k_arr, k_col in k_parts:
            in_specs.append(pl.BlockSpec((keys_blk, LANES),
                                         lambda b, g, t, blk_fn=blk_fn, k_col=k_col: (blk_fn(b), k_col(g))))
            args.append(k_arr)
        in_specs.append(pl.BlockSpec((LANES, keys_blk),
                                     lambda b, g, t, blk_fn=blk_fn, vt_row=vt_row: (vt_row(g), blk_fn(b))))
        args.append(vt_arr)
    out_spec = pl.BlockSpec((tq, heads * LANES), lambda b, g, t: (out_rows(b, t), g))
    kern = functools.partial(_attn_kernel, n_parts=n_parts, n_segs=len(key_segs), heads=heads, tq=tq)
    total_keys = sum(seg[0] for seg in key_segs)
    return pl.pallas_call(
        kern, grid=(n_batch, n_groups, n_qt), in_specs=in_specs, out_specs=out_spec,
        scratch_shapes=[pltpu.VMEM((total_keys, heads * tq), F32),
                        pltpu.VMEM((LANES + 2 * SUBLANES, heads * tq), F32)],
        out_shape=jax.ShapeDtypeStruct((out_rows_total, n_groups * heads * LANES), BF16),
        compiler_params=_params(("parallel", "parallel", "arbitrary")))(*args)


def _attend_both(cfg, q_arr, k_parts, vt_part, n_parts, heads, n_groups, tq, ctx_out):
    tq = min(tq, cfg.L)
    n_qt = cfg.L // tq
    lat_blocks_ctx = cfg.NL // cfg.CL
    seg_lat = (cfg.L, lambda b: b, k_parts, vt_part)
    seg_ctx = (cfg.CL, lambda b: lat_blocks_ctx + b, k_parts, vt_part)
    lat_rows = lambda b, t: b * n_qt + t
    y_lat = attention(q_arr, lat_rows, lat_rows, [seg_ctx, seg_lat], n_parts=n_parts, heads=heads,
                      n_batch=cfg.B, n_groups=n_groups, n_qt=n_qt, tq=tq, out_rows_total=cfg.NL)
    if not ctx_out:
        return y_lat
    y_ctx = attention(q_arr, lambda b, t: lat_blocks_ctx + b, lambda b, t: b, [seg_ctx], n_parts=n_parts,
                      heads=heads, n_batch=cfg.B, n_groups=n_groups, n_qt=1, tq=cfg.CL, out_rows_total=cfg.NC)
    return jnp.concatenate([y_lat, y_ctx], axis=0)


def _rope_tables(cfg, rot_dim, half_width):
    rows = cfg.L // GRID_W
    row = jnp.repeat(jnp.arange(rows, dtype=F32), GRID_W)
    col = jnp.tile(jnp.arange(GRID_W, dtype=F32), rows)
    quarter = rot_dim // 4
    inv_freq = ROPE_THETA ** (-jnp.arange(quarter, dtype=F32) / quarter)
    ang = jnp.concatenate([row[:, None] * inv_freq, col[:, None] * inv_freq], -1)
    c, s = jnp.cos(ang), jnp.sin(ang)
    pad = half_width - rot_dim // 2
    one = jnp.ones((cfg.L, pad), F32)
    zero = jnp.zeros((cfg.L, pad), F32)
    cos_t = jnp.concatenate([c, one, c, one], -1)
    sin_t = jnp.concatenate([-s, zero, s, zero], -1)
    cos_t = jnp.concatenate([jnp.tile(cos_t, (cfg.B, 1)), jnp.ones((cfg.NC, LANES), F32)], 0)
    sin_t = jnp.concatenate([jnp.tile(sin_t, (cfg.B, 1)), jnp.zeros((cfg.NC, LANES), F32)], 0)
    return cos_t, sin_t


def _deinterleave_cols(w, rot_dim):
    half = rot_dim // 2
    pad = LANES // 2 - half
    ev = w[..., 0::2]
    od = w[..., 1::2]
    z = jnp.zeros(w.shape[:-1] + (pad,), w.dtype)
    return jnp.concatenate([ev, z, od, z], -1)


def mla_layer(cfg, xa, mod, w_dq, q_norm, w_uq, w_dkv, kv_norm, w_ukv, w_o, lng, lnb, ctx_out):
    d = cfg.D
    tm = cfg.tm
    seg = _seg_fn(cfg)
    scale = (MLA_NOPE + MLA_ROPE) ** -0.5
    cos_t, sin_t = _rope_tables(cfg, MLA_ROPE, LANES // 2)
    hq = MLA_HEADS

    w_uq3 = w_uq.reshape(MLA_Q_RANK, hq, MLA_NOPE + MLA_ROPE)
    w_uq_p = jnp.concatenate([w_uq3[..., :MLA_NOPE], _deinterleave_cols(w_uq3[..., MLA_NOPE:], MLA_ROPE)], -1)
    w_uq_p = w_uq_p.reshape(MLA_Q_RANK, hq * 2 * LANES).astype(BF16)
    w_dkv_p = jnp.concatenate([w_dkv[:, :MLA_KV_RANK], _deinterleave_cols(w_dkv[:, MLA_KV_RANK:], MLA_ROPE)],
                              -1).astype(BF16)

    def epi_cq(acc, rows, cols, m):
        return [_rms(acc, cols[0])]

    cq = fused_linear(xa, w_dq.astype(BF16), tm=tm, tn=MLA_Q_RANK, out_defs=[(MLA_Q_RANK, MLA_Q_RANK, BF16)],
                      epilogue=epi_cq, seg_fn=seg, mod=mod, mod_rows=(0, 1),
                      col_extras=[q_norm.reshape(1, -1)])[0]

    heads_per_tile = 4

    def epi_q(acc, rows, cols, m):
        parts = []
        for hh in range(heads_per_tile):
            c0 = hh * 2 * LANES
            parts.append(acc[:, c0:c0 + LANES] * scale)
            parts.append(_rope(acc[:, c0 + LANES:c0 + 2 * LANES] * scale, rows[0], rows[1]))
        return [jnp.concatenate(parts, -1)]

    tn_q = heads_per_tile * 2 * LANES
    q = fused_linear(cq, w_uq_p, tm=tm, tn=tn_q, out_defs=[(hq * 2 * LANES, tn_q, BF16)],
                     epilogue=epi_q, row_extras=[cos_t, sin_t])[0]

    def epi_ckv(acc, rows, cols, m):
        return [_rms(acc[:, :MLA_KV_RANK], cols[0][:, :MLA_KV_RANK]),
                _rope(acc[:, MLA_KV_RANK:], rows[0], rows[1])]

    gain_pad = jnp.concatenate([kv_norm, jnp.ones((LANES,), F32)]).reshape(1, -1)
    ckv, kpe = fused_linear(xa, w_dkv_p, tm=tm, tn=MLA_KV_RANK + LANES,
                            out_defs=[(MLA_KV_RANK, MLA_KV_RANK, BF16), (LANES, LANES, BF16)],
                            epilogue=epi_ckv, seg_fn=seg, mod=mod, mod_rows=(0, 1),
                            row_extras=[cos_t, sin_t], col_extras=[gain_pad])
    n_kv = w_ukv.shape[1]
    kv = fused_linear(ckv, w_ukv.astype(BF16), tm=tm, tn=n_kv, out_defs=[(n_kv, n_kv, BF16)],
                      epilogue=lambda acc, rows, cols, m: [acc])[0]

    vt = kv.reshape(cfg.NT, hq, 2, LANES)[:, :, 1, :].reshape(cfg.NT, hq * LANES).T
    y = _attend_both(cfg, q, [(kv, lambda h: 2 * h), (kpe, lambda h: 0)], (vt, lambda h: h),
                     n_parts=2, heads=1, n_groups=hq, tq=512, ctx_out=ctx_out)
    n_rows = None if ctx_out else cfg.NL
    return residual_ln_linear(cfg, xa, y, w_o.astype(BF16), mod, 2, lng, lnb, n_rows=n_rows)


def gqa_layer(cfg, xa, mod, w_qkv, q_norm, k_norm, w_o, lng, lnb, ctx_out):
    tm = cfg.tm
    seg = _seg_fn(cfg)
    hd = GQA_HEAD_DIM
    nq, nk = GQA_Q_HEADS, GQA_KV_HEADS
    scale = hd ** -0.5
    cos_t, sin_t = _rope_tables(cfg, hd, LANES // 2)
    perm = np.concatenate([np.arange(0, hd, 2), np.arange(1, hd, 2)])
    w3 = w_qkv.reshape(cfg.D, nq + 2 * nk, hd)
    w_p = jnp.concatenate([w3[:, :nq + nk, perm], w3[:, nq + nk:, :]], 1).reshape(cfg.D, -1).astype(BF16)
    gains = jnp.stack([q_norm[perm], k_norm[perm]])
    n_chunks = nq + 2 * nk

    def epi(acc, rows, cols, m):
        g = cols[0]
        outs = []
        for c in range(n_chunks):
            y = acc[:, c * hd:(c + 1) * hd]
            if c < nq:
                y = _rope(_rms(y, g[0:1, :hd]) * scale, rows[0], rows[1])
            elif c < nq + nk:
                y = _rope(_rms(y, g[1:2, :hd]), rows[0], rows[1])
            outs.append(y)
        return [jnp.concatenate(outs, -1)]

    n_out = n_chunks * hd
    gains_full = jnp.tile(gains, (1, n_chunks))
    qkv = fused_linear(xa, w_p, tm=tm, tn=n_out, out_defs=[(n_out, n_out, BF16)], epilogue=epi,
                       seg_fn=seg, mod=mod, mod_rows=(0, 1), row_extras=[cos_t, sin_t],
                       col_extras=[gains_full])[0]
    grp = nq // nk
    vt = qkv[:, (nq + nk) * hd:].T
    y = _attend_both(cfg, qkv, [(qkv, lambda g: nq + g)], (vt, lambda g: g),
                     n_parts=1, heads=grp, n_groups=nk, tq=128, ctx_out=ctx_out)
    n_rows = None if ctx_out else cfg.NL
    return residual_ln_linear(cfg, xa, y, w_o.astype(BF16), mod, 2, lng, lnb, n_rows=n_rows)


PEER_E1_PER_STEP = 16
PEER_E1_PER_DOT = 4
GATE_STEP = 128.0
_PAIRS = [(a, b) for a in range(PEER_TOPK) for b in range(PEER_TOPK) if (a + 1) * (b + 1) <= PEER_TOPK]


def _insert_sorted(lst, x):
    out = []
    for a in lst:
        out.append(jnp.maximum(a, x))
        x = jnp.minimum(a, x)
    return out


def _gelu(x):
    return 0.5 * x * (1.0 + lax.erf(x * (2.0 ** -0.5)))


def _peer_route_group(s1_scr, s2_scr, cnt_scr, r2_scr, g):
    nk = PEER_N_KEYS
    k = PEER_TOPK
    neg = jnp.full((PEER_HEADS, LANES), -jnp.inf, F32)

    def key_rows(i):
        return pl.ds(pl.multiple_of(i * PEER_HEADS, PEER_HEADS), PEER_HEADS)

    def top_values(load):
        return lax.fori_loop(0, nk, lambda i, lst: tuple(_insert_sorted(lst, load(i))), (neg,) * k, unroll=4)

    v1 = top_values(lambda i: s1_scr[g, key_rows(i), :])
    v2 = top_values(lambda i: s2_scr[g, key_rows(i), :])
    cand = {ab: v1[ab[0]] + v2[ab[1]] for ab in _PAIRS}
    top = [neg] * k
    for ab in _PAIRS:
        top = _insert_sorted(top, cand[ab])
    tau = top[k - 1]
    cmax = cand[(0, 0)]
    zsum = jnp.zeros_like(tau)
    for ab in _PAIRS:
        zsum = zsum + jnp.where(cand[ab] >= tau, jnp.exp(cand[ab] - cmax), 0.0)
    inv_z = 1.0 / zsum
    m1, m2 = v1[0], v2[0]

    def finish(i, carry):
        rows = key_rows(i)
        x1 = s1_scr[g, rows, :]
        x2 = s2_scr[g, rows, :]
        cnt = jnp.zeros_like(x1)
        r2 = jnp.zeros_like(x2)
        for b in range(k):
            cnt = cnt + jnp.where(x1 + v2[b] >= tau, 1.0, 0.0)
            r2 = r2 + jnp.where(v2[b] > x2, 1.0, 0.0)
        cnt_scr[g, rows, :] = cnt * GATE_STEP
        r2_scr[g, rows, :] = r2 * GATE_STEP
        s1_scr[g, rows, :] = jnp.exp(x1 - m1) * inv_z
        s2_scr[g, rows, :] = jnp.exp(x2 - m2)
        return carry

    lax.fori_loop(0, nk, finish, 0, unroll=2)


def _peer_kernel(x_ref, mod_ref, wq_ref, k1_ref, k2_ref, *rest, tokens):
    n_sub = PEER_E1_PER_STEP // PEER_E1_PER_DOT
    u_refs, vt_refs = rest[:n_sub], rest[n_sub:2 * n_sub]
    (lng_ref, lnb_ref, o_ref, ht_scr, s1_scr, s2_scr, cnt_scr, r2_scr, r2b_scr, p2b_scr, act_scr,
     acc_scr) = rest[2 * n_sub:]
    e = pl.program_id(1)
    n_groups = tokens // LANES
    nh = PEER_HEADS
    nk = PEER_N_KEYS
    pack = 2 * SUBLANES

    @pl.when(e == 0)
    def _route():
        mod = mod_ref[...]
        h = x_ref[...] * (1.0 + mod[4:5, :]) + mod[3:4, :]
        ht = h.T.astype(BF16)
        ht_scr[...] = ht
        qt = jnp.dot(wq_ref[...], ht, preferred_element_type=F32)
        half_rows = nh * (PEER_D_KEY // 2)
        s1 = jnp.dot(k1_ref[...], qt[:half_rows].astype(BF16), preferred_element_type=F32)
        s2 = jnp.dot(k2_ref[...], qt[half_rows:].astype(BF16), preferred_element_type=F32)
        for g in range(n_groups):
            s1_scr[g] = s1[:, g * LANES:(g + 1) * LANES]
            s2_scr[g] = s2[:, g * LANES:(g + 1) * LANES]
        for g in range(n_groups):
            _peer_route_group(s1_scr, s2_scr, cnt_scr, r2_scr, g)
        for g in range(n_groups):
            for hd in range(nh):
                r2b_scr[g, hd] = r2_scr[g, pl.ds(hd, nk, stride=nh), :].astype(BF16)
                p2b_scr[g, hd] = s2_scr[g, pl.ds(hd, nk, stride=nh), :].astype(BF16)
        acc_scr[...] = jnp.zeros_like(acc_scr)

    def up_projection(k):
        act = jnp.dot(u_refs[k][...], ht_scr[...], preferred_element_type=F32)
        for g in range(n_groups):
            act_scr[k % 2, g] = act[:, g * LANES:(g + 1) * LANES]

    up_projection(0)
    for k in range(n_sub):
        if k + 1 < n_sub:
            up_projection(k + 1)
        row0 = pl.multiple_of((e * PEER_E1_PER_STEP + k * PEER_E1_PER_DOT) * nh, nh)
        w_cols = []
        for g in range(n_groups):
            cnt = cnt_scr[g, pl.ds(row0, PEER_E1_PER_DOT * nh), :]
            p1 = s1_scr[g, pl.ds(row0, PEER_E1_PER_DOT * nh), :]
            gates = [jnp.zeros((nk // pack, pack, LANES), BF16) for _ in range(PEER_E1_PER_DOT)]
            for hd in range(nh):
                r2t = r2b_scr[g, hd].reshape(nk // pack, pack, LANES)
                p2t = p2b_scr[g, hd].reshape(nk // pack, pack, LANES)
                for jj in range(PEER_E1_PER_DOT):
                    r = jj * nh + hd
                    cb = jnp.broadcast_to(cnt[r:r + 1, :], (pack, LANES)).astype(BF16)
                    pb = jnp.broadcast_to(p1[r:r + 1, :], (pack, LANES)).astype(BF16)
                    gates[jj] = gates[jj] + jnp.minimum(jnp.maximum(cb[None] - r2t, 0), pb[None]) * p2t
            w_parts = []
            for jj in range(PEER_E1_PER_DOT):
                a = _gelu(act_scr[k % 2, g, pl.ds(jj * nk, nk), :].astype(BF16))
                w_parts.append(gates[jj].reshape(nk, LANES) * a)
            w_cols.append(jnp.concatenate(w_parts, axis=0))
        w = jnp.concatenate(w_cols, axis=1)
        acc_scr[...] += jnp.dot(vt_refs[k][...], w, preferred_element_type=F32)

    @pl.when(e == pl.num_programs(1) - 1)
    def _finish():
        mod = mod_ref[...]
        f = acc_scr[...].T
        v = DEEPNORM_ALPHA * x_ref[...] + mod[5:6, :] * f
        o_ref[...] = _layer_norm(v, lng_ref[...], lnb_ref[...])


def peer_layer(cfg, xa, mod, layer, wq_t, k1_bd, k2_bd, u_b, vt_b, lng, lnb, n_rows):
    d = cfg.D
    t = cfg.tm
    n_exp = u_b.shape[1]
    e_tile = PEER_E1_PER_STEP * PEER_N_KEYS
    seg = _seg_fn(cfg)
    nq = wq_t.shape[1]
    nkh = k1_bd.shape[1]
    ng = t // LANES
    in_specs = [
        pl.BlockSpec((t, d), lambda i, e: (i, 0)),
        pl.BlockSpec((None, 6, d), lambda i, e: (seg(i), 0, 0)),
        pl.BlockSpec((None, nq, d), lambda i, e: (layer, 0, 0), pipeline_mode=pl.Buffered(1)),
        pl.BlockSpec((None, nkh, nkh), lambda i, e: (layer, 0, 0), pipeline_mode=pl.Buffered(1)),
        pl.BlockSpec((None, nkh, nkh), lambda i, e: (layer, 0, 0), pipeline_mode=pl.Buffered(1)),
    ]
    n_sub = PEER_E1_PER_STEP // PEER_E1_PER_DOT
    sub_rows = PEER_E1_PER_DOT * PEER_N_KEYS
    in_specs += [pl.BlockSpec((None, sub_rows, d), lambda i, e, k=k: (layer, e * n_sub + k, 0))
                 for k in range(n_sub)]
    in_specs += [pl.BlockSpec((None, None, d, sub_rows), lambda i, e, k=k: (layer, e * n_sub + k, 0, 0))
                 for k in range(n_sub)]
    in_specs += [
        pl.BlockSpec((1, d), lambda i, e: (0, 0)),
        pl.BlockSpec((1, d), lambda i, e: (0, 0)),
    ]
    scratch = [
        pltpu.VMEM((d, t), BF16),
        pltpu.VMEM((ng, nkh, LANES), F32),
        pltpu.VMEM((ng, nkh, LANES), F32),
        pltpu.VMEM((ng, nkh, LANES), F32),
        pltpu.VMEM((ng, nkh, LANES), F32),
        pltpu.VMEM((ng, PEER_HEADS, PEER_N_KEYS, LANES), BF16),
        pltpu.VMEM((ng, PEER_HEADS, PEER_N_KEYS, LANES), BF16),
        pltpu.VMEM((2, ng, PEER_E1_PER_DOT * PEER_N_KEYS, LANES), F32),
        pltpu.VMEM((d, t), F32),
    ]
    return pl.pallas_call(
        functools.partial(_peer_kernel, tokens=t),
        grid=(n_rows // t, n_exp // e_tile), in_specs=in_specs,
        out_specs=pl.BlockSpec((t, d), lambda i, e: (i, 0)),
        out_shape=jax.ShapeDtypeStruct((n_rows, d), F32), scratch_shapes=scratch,
        compiler_params=_params(("parallel", "arbitrary")))(
            xa, mod, wq_t, k1_bd, k2_bd, *([u_b] * n_sub), *([vt_b] * n_sub), lng.reshape(1, d), lnb.reshape(1, d))


def peer_prepare(peer_w_q, peer_keys1, peer_keys2, peer_u, peer_v):
    depth, d, _ = peer_w_q.shape
    half = PEER_D_KEY // 2
    nh = PEER_HEADS
    wq_t = peer_w_q.reshape(depth, d, nh, 2, half).transpose(0, 3, 2, 4, 1).reshape(depth, 2 * nh * half, d)
    eye = jnp.eye(nh, dtype=F32)

    def bd(keys):
        return jnp.einsum('lhkd,hg->lkhgd', keys, eye).reshape(depth, PEER_N_KEYS * nh, nh * half).astype(BF16)

    sub_rows = PEER_E1_PER_DOT * PEER_N_KEYS
    n_exp = peer_v.shape[1]
    vt = peer_v.astype(BF16).reshape(depth, n_exp // sub_rows, sub_rows, d).transpose(0, 1, 3, 2)
    return (wq_t.astype(BF16), bd(peer_keys1), bd(peer_keys2), peer_u.astype(BF16), vt)


HY_C_TILE = LANES
ROW_PAD = SUBLANES
FFT_UNROLL = 8


def _fft_sizes(length):
    n = 2 * length
    n1 = 64 if n >= 8192 else (32 if n >= 1024 else 16)
    return n1, n // n1


def _fft_mats(n1, n2):
    n = n1 * n2
    two_pi = 2.0 * math.pi
    k2 = jnp.arange(n2, dtype=jnp.int32)
    i1 = jnp.arange(n1, dtype=jnp.int32)
    ang = two_pi * ((k2[:, None] * k2[None, :]) % n2).astype(F32) / n2
    c1 = jnp.concatenate([jnp.cos(ang), -jnp.sin(ang)], 0)
    m = (i1[None, None, :] * k2[:, None, None] + i1[None, None, :] * i1[None, :, None] * n2) % n
    ang = two_pi * m.astype(F32) / n
    gr, gi = jnp.cos(ang), -jnp.sin(ang)
    g = jnp.concatenate([jnp.concatenate([gr, -gi], -1), jnp.concatenate([gi, gr], -1)], -2)
    hr, hi = jnp.swapaxes(gr, 1, 2), -jnp.swapaxes(gi, 1, 2)
    h = jnp.concatenate([jnp.concatenate([hr, -hi], -1), jnp.concatenate([hi, hr], -1)], -2)
    ang = two_pi * ((k2[:n2 // 2, None] * k2[None, :]) % n2).astype(F32) / n2
    c2 = jnp.concatenate([jnp.cos(ang), -jnp.sin(ang)], 1) / n
    return c1.astype(BF16), g.astype(BF16), h.astype(BF16), c2.astype(BF16)


def _fft_stage1(src, c1, a_scr, n1, n2, n2_in):
    pitch = 2 * n2 + ROW_PAD

    def body(i, carry):
        xs = src[pl.ds(i, n2_in, stride=n1), :].astype(BF16)
        a_scr[pl.ds(pl.multiple_of(i * pitch, SUBLANES), 2 * n2), :] = jnp.dot(c1, xs, preferred_element_type=F32)
        return carry

    lax.fori_loop(0, n1, body, 0, unroll=FFT_UNROLL)


def _fft_stage2_load(a_scr, k, n1, n2):
    pitch = 2 * n2 + ROW_PAD
    ar = a_scr[pl.ds(k, n1, stride=pitch), :]
    ai = a_scr[pl.ds(n2 + k, n1, stride=pitch), :]
    return jnp.concatenate([ar, ai], 0).astype(BF16)


def _short_conv(z, w, b):
    rows = z.shape[0]
    idx = lax.broadcasted_iota(jnp.int32, z.shape, 0)
    prev = jnp.where(idx == 0, 0.0, pltpu.roll(z, 1, 0))
    nxt = jnp.where(idx == rows - 1, 0.0, pltpu.roll(z, rows - 1, 0))
    return b + prev * w[0:1, :] + z * w[1:2, :] + nxt * w[2:3, :]


def _hyena_conv_kernel(*refs, first, n1, n2):
    if first:
        (zu_ref, cw_u_ref, cb_u_ref, zx_ref, cw_x_ref, cb_x_ref, bias_ref, sr_ref, si_ref,
         c1_ref, g_ref, h_ref, c2_ref, o_ref, u_scr, y_scr, a_scr, b_scr) = refs
        u_scr[...] = _short_conv(zu_ref[...].astype(F32), cw_u_ref[...], cb_u_ref[...])
    else:
        (zu_ref, zx_ref, cw_x_ref, cb_x_ref, bias_ref, sr_ref, si_ref,
         c1_ref, g_ref, h_ref, c2_ref, o_ref, u_scr, y_scr, a_scr, b_scr) = refs
        u_scr[...] = zu_ref[...]
    _fft_stage1(u_scr, c1_ref[...], a_scr, n1, n2, n2 // 2)
    pitch_b = 2 * n1 + ROW_PAD

    def mid(k, carry):
        x = jnp.dot(g_ref[k], _fft_stage2_load(a_scr, k, n1, n2), preferred_element_type=F32)
        rows = pl.ds(pl.multiple_of(k * n1, n1), n1)
        xr, xi = x[:n1], x[n1:]
        sr, si = sr_ref[rows, :], si_ref[rows, :]
        y = jnp.concatenate([xr * sr - xi * si, xr * si + xi * sr], 0).astype(BF16)
        b_scr[pl.ds(pl.multiple_of(k * pitch_b, SUBLANES), 2 * n1), :] = jnp.dot(
            h_ref[k], y, preferred_element_type=F32)
        return carry

    lax.fori_loop(0, n2, mid, 0, unroll=FFT_UNROLL)

    def last(i, carry):
        br = b_scr[pl.ds(i, n2, stride=pitch_b), :]
        bi = b_scr[pl.ds(n1 + i, n2, stride=pitch_b), :]
        y = jnp.dot(c2_ref[...], jnp.concatenate([br, bi], 0).astype(BF16), preferred_element_type=F32)
        y_scr[pl.ds(i, n2 // 2, stride=n1), :] = y
        return carry

    lax.fori_loop(0, n1, last, 0, unroll=FFT_UNROLL)
    x = _short_conv(zx_ref[...].astype(F32), cw_x_ref[...], cb_x_ref[...])
    u = u_scr[...]
    o_ref[...] = (x * (y_scr[...] + u * bias_ref[...])).astype(o_ref.dtype)


def _hyena_conv(y_prev, z, zu_col0, zx_col0, conv_w, conv_b, bias_row, spec_r, spec_i, spec_col0, mats,
                length, n_seq, row_blk0, out_dtype):
    n1, n2 = _fft_sizes(length)
    c1, g, h, c2 = mats
    c1 = c1[:, :n2 // 2]
    ct = HY_C_TILE
    d = bias_row.shape[1]
    n = 2 * length
    one = pl.Buffered(1)
    first = y_prev is None

    def z_specs(col0):
        return [pl.BlockSpec((length, ct), lambda c, s: (row_blk0 + s, col0 + c)),
                pl.BlockSpec((HY_SHORT, ct), lambda c, s: (0, col0 + c)),
                pl.BlockSpec((1, ct), lambda c, s: (0, col0 + c))]

    if first:
        in_specs = z_specs(zu_col0)
        args = [z, conv_w, conv_b]
    else:
        in_specs = [pl.BlockSpec((length, ct), lambda c, s: (s, c))]
        args = [y_prev]
    in_specs += z_specs(zx_col0) + [
        pl.BlockSpec((1, ct), lambda c, s: (0, c)),
        pl.BlockSpec((n, ct), lambda c, s: (0, spec_col0 + c), pipeline_mode=one),
        pl.BlockSpec((n, ct), lambda c, s: (0, spec_col0 + c), pipeline_mode=one),
        pl.BlockSpec(c1.shape, lambda c, s: (0, 0), pipeline_mode=one),
        pl.BlockSpec(g.shape, lambda c, s: (0, 0, 0), pipeline_mode=one),
        pl.BlockSpec(h.shape, lambda c, s: (0, 0, 0), pipeline_mode=one),
        pl.BlockSpec(c2.shape, lambda c, s: (0, 0), pipeline_mode=one),
    ]
    args += [z, conv_w, conv_b, bias_row, spec_r, spec_i, c1, g, h, c2]
    scratch = [
        pltpu.VMEM((length, ct), F32),
        pltpu.VMEM((length, ct), F32),
        pltpu.VMEM((n1 * (2 * n2 + ROW_PAD), ct), F32),
        pltpu.VMEM((n2 * (2 * n1 + ROW_PAD), ct), F32),
    ]
    return pl.pallas_call(
        functools.partial(_hyena_conv_kernel, first=first, n1=n1, n2=n2),
        grid=(d // ct, n_seq), in_specs=in_specs,
        out_specs=pl.BlockSpec((length, ct), lambda c, s: (s, c)),
        out_shape=jax.ShapeDtypeStruct((n_seq * length, d), out_dtype), scratch_shapes=scratch,
        compiler_params=_params(("parallel", "arbitrary")))(*args)


def _hyena_filter_kernel(emb_ref, w1_ref, b1_ref, fr1_ref, w2_ref, b2_ref, fr2_ref, w3_ref, dl_ref, o_ref):
    hi = lax.Precision.HIGHEST
    emb = emb_ref[...]
    hdn = jnp.sin(fr1_ref[...] * (jnp.dot(emb, w1_ref[...], precision=hi, preferred_element_type=F32) + b1_ref[...]))
    hdn = jnp.sin(fr2_ref[...] * (jnp.dot(hdn, w2_ref[...], precision=hi, preferred_element_type=F32) + b2_ref[...]))
    filt = jnp.dot(hdn, w3_ref[...], precision=hi, preferred_element_type=F32)
    t01 = emb[:, 0:1]
    mask = emb[:, LANES - 1:LANES]
    window = jnp.exp(-t01 * dl_ref[...]) + HY_SHIFT
    o_ref[...] = filt * window * mask


def _hyena_spec_kernel(k_ref, c1_ref, g_ref, sr_ref, si_ref, a_scr, *, n1, n2):
    _fft_stage1(k_ref, c1_ref[...], a_scr, n1, n2, n2)

    def mid(k, carry):
        x = jnp.dot(g_ref[k], _fft_stage2_load(a_scr, k, n1, n2), preferred_element_type=F32)
        rows = pl.ds(pl.multiple_of(k * n1, n1), n1)
        sr_ref[rows, :] = x[:n1]
        si_ref[rows, :] = x[n1:]
        return carry

    lax.fori_loop(0, n2, mid, 0, unroll=FFT_UNROLL)


def _hyena_spectra(length, d, f_w1, f_b1, f_fr1, f_w2, f_b2, f_fr2, f_w3, mats):
    n = 2 * length
    n1, n2 = _fft_sizes(length)
    hid = f_w1.shape[1]
    lag = jnp.concatenate([jnp.arange(length), jnp.zeros((1,), jnp.int32), jnp.arange(length - 1, 0, -1)])
    t01 = jnp.linspace(0.0, 1.0, length, dtype=F32)[:, None]
    w = 2.0 * math.pi * jnp.arange(length, dtype=F32)[:, None] / length
    f = jnp.linspace(1e-4, HY_BANDS - 1, HY_BANDS, dtype=F32)[None, :]
    emb = jnp.concatenate([t01, jnp.cos(f * w), -jnp.sin(f * w)], -1)[lag]
    mask = jnp.ones((n, 1), F32).at[length, 0].set(0.0)
    emb = jnp.concatenate([emb, jnp.zeros((n, LANES - HY_EMB - 1), F32), mask], -1)

    def pad2(a, r, c):
        return jnp.zeros((r, c), F32).at[:a.shape[0], :a.shape[1]].set(a)

    w1 = pad2(f_w1, LANES, LANES)
    b1 = pad2(f_b1[None], 1, LANES)
    fr1 = pad2(f_fr1[None], 1, LANES)
    w2 = pad2(f_w2, LANES, LANES)
    b2 = pad2(f_b2[None], 1, LANES)
    fr2 = pad2(f_fr2[None], 1, LANES)
    w3 = pad2(f_w3, LANES, f_w3.shape[1])
    deltas = jnp.abs(jnp.linspace(math.log(HY_TARGET) / HY_SLOW_DECAY, math.log(HY_TARGET) / HY_FAST_DECAY,
                                  d, dtype=F32))[None, :]
    tr = min(512, length)
    half_tiles = length // tr
    small = lambda shape: pl.BlockSpec(shape, lambda r, o: (0, 0))
    kern = pl.pallas_call(
        _hyena_filter_kernel, grid=(n // tr, HY_ORDER),
        in_specs=[pl.BlockSpec((tr, LANES), lambda r, o: (r, 0)),
                  small((LANES, LANES)), small((1, LANES)), small((1, LANES)),
                  small((LANES, LANES)), small((1, LANES)), small((1, LANES)),
                  pl.BlockSpec((LANES, d), lambda r, o: (0, 2 * o + r // half_tiles)),
                  small((1, d))],
        out_specs=pl.BlockSpec((tr, d), lambda r, o: (r, o)),
        out_shape=jax.ShapeDtypeStruct((n, HY_ORDER * d), F32),
        compiler_params=_params(("parallel", "parallel")))(emb, w1, b1, fr1, w2, b2, fr2, w3, deltas)
    c1, g, _, _ = mats
    ct = HY_C_TILE
    n_cols = HY_ORDER * d
    one = pl.Buffered(1)
    spec_r, spec_i = pl.pallas_call(
        functools.partial(_hyena_spec_kernel, n1=n1, n2=n2), grid=(n_cols // ct,),
        in_specs=[pl.BlockSpec((n, ct), lambda c: (0, c)),
                  pl.BlockSpec(c1.shape, lambda c: (0, 0), pipeline_mode=one),
                  pl.BlockSpec(g.shape, lambda c: (0, 0, 0), pipeline_mode=one)],
        out_specs=[pl.BlockSpec((n, ct), lambda c: (0, c)), pl.BlockSpec((n, ct), lambda c: (0, c))],
        out_shape=[jax.ShapeDtypeStruct((n, n_cols), F32)] * 2,
        scratch_shapes=[pltpu.VMEM((n1 * (2 * n2 + ROW_PAD), ct), F32)],
        compiler_params=_params(("parallel",)))(kern, c1, g)
    return spec_r, spec_i


def hyena_layer(cfg, xa, mod, w_in, b_in, conv_w, conv_b, f_w1, f_b1, f_fr1, f_w2, f_b2, f_fr2, f_w3,
                bias, w_out, b_out, lng, lnb, ctx_out):
    d = cfg.D
    n_ct = d // HY_C_TILE
    z = fused_linear(xa, w_in.astype(BF16), tm=cfg.tm, tn=3 * d // 2, out_defs=[(3 * d, 3 * d // 2, BF16)],
                     epilogue=lambda acc, rows, cols, m: [acc + cols[0]], seg_fn=_seg_fn(cfg), mod=mod,
                     mod_rows=(0, 1), col_extras=[b_in.reshape(1, -1)],
                     n_rows=None if ctx_out else cfg.NL)[0]
    conv_b2 = conv_b.reshape(1, -1)
    seqs = [(cfg.L, cfg.B, 0)]
    if ctx_out:
        seqs.append((cfg.CL, cfg.B, cfg.NL // cfg.CL))
    ys = []
    for length, n_seq, row_blk0 in seqs:
        mats = _fft_mats(*_fft_sizes(length))
        spec_r, spec_i = _hyena_spectra(length, d, f_w1, f_b1, f_fr1, f_w2, f_b2, f_fr2, f_w3, mats)
        y1 = _hyena_conv(None, z, 2 * n_ct, 0, conv_w, conv_b2, bias[0:1], spec_r, spec_i, 0,
                         mats, length, n_seq, row_blk0, F32)
        ys.append(_hyena_conv(y1, z, 0, n_ct, conv_w, conv_b2, bias[1:2], spec_r, spec_i, n_ct,
                              mats, length, n_seq, row_blk0, BF16))
    y = ys[0] if len(ys) == 1 else jnp.concatenate(ys, 0)
    n_rows = None if ctx_out else cfg.NL
    return residual_ln_linear(cfg, xa, y, w_out.astype(BF16), mod, 2, lng, lnb, bias=b_out, n_rows=n_rows)


def ada_modulation(cfg, cvec, ada_w, ada_b, layer):
    d = cfg.D
    out = fused_linear(cvec, ada_w, w_layer=layer, tm=cvec.shape[0], tn=d, out_defs=[(6 * d, d, F32)],
                       epilogue=lambda acc, rows, cols, m: [acc + cols[0]], pre=jax.nn.silu,
                       col_extras=[ada_b[layer].reshape(1, -1)])[0]
    return out[:cfg.B + 1].reshape(cfg.B + 1, 6, d)


def kernel(x, c, ctx, c_ctx, ada_w, ada_b, ln_g, ln_b, mla_w_dq, mla_q_norm, mla_w_uq, mla_w_dkv, mla_kv_norm,
           mla_w_ukv, mla_w_o, hy_w_in, hy_b_in, hy_conv_w, hy_conv_b, hy_f_w1, hy_f_b1, hy_f_fr1, hy_f_w2,
           hy_f_b2, hy_f_fr2, hy_f_w3, hy_bias, hy_w_out, hy_b_out, gqa_w_qkv, gqa_q_norm, gqa_k_norm, gqa_w_o,
           peer_w_q, peer_keys1, peer_keys2, peer_u, peer_v):
    batch, seq, d = x.shape
    cfg = Cfg(batch, seq, ctx.shape[1], d)
    xa = jnp.concatenate([x.reshape(-1, d), ctx.reshape(-1, d)], 0)
    pad_rows = -(batch + 1) % SUBLANES
    cvec = jnp.concatenate([c, c_ctx[None, :], jnp.zeros((pad_rows, d), F32)], 0)
    peer_w = peer_prepare(peer_w_q, peer_keys1, peer_keys2, peer_u, peer_v)
    ia = ib = ic = 0
    for layer in range(DEPTH):
        last = layer == DEPTH - 1
        mod = ada_modulation(cfg, cvec, ada_w, ada_b, layer)
        lng, lnb = ln_g[layer, 0], ln_b[layer, 0]
        kind = layer % N_MIXERS
        if kind == 0:
            xm = mla_layer(cfg, xa, mod, mla_w_dq[ia], mla_q_norm[ia], mla_w_uq[ia], mla_w_dkv[ia],
                           mla_kv_norm[ia], mla_w_ukv[ia], mla_w_o[ia], lng, lnb, not last)
            ia += 1
        elif kind == 1:
            xm = hyena_layer(cfg, xa, mod, hy_w_in[ib], hy_b_in[ib], hy_conv_w[ib], hy_conv_b[ib], hy_f_w1[ib],
                             hy_f_b1[ib], hy_f_fr1[ib], hy_f_w2[ib], hy_f_b2[ib], hy_f_fr2[ib], hy_f_w3[ib],
                             hy_bias[ib], hy_w_out[ib], hy_b_out[ib], lng, lnb, not last)
            ib += 1
        else:
            xm = gqa_layer(cfg, xa, mod, gqa_w_qkv[ic], gqa_q_norm[ic], gqa_k_norm[ic], gqa_w_o[ic],
                           lng, lnb, not last)
            ic += 1
        xa = peer_layer(cfg, xm, mod, layer, *peer_w, ln_g[layer, 1], ln_b[layer, 1], xm.shape[0])
    return xa[:cfg.NL].reshape(batch, seq, d)
```

```python
import functools
import math

import numpy as np
import jax
import jax.numpy as jnp
from jax import lax
from jax.experimental import pallas as pl
from jax.experimental.pallas import tpu as pltpu

F32 = jnp.float32
BF16 = jnp.bfloat16

DEPTH = 4
GRID_W = 64
N_MIXERS = 3
DEEPNORM_ALPHA = (2 * DEPTH) ** 0.25
LN_EPS = 1e-5
RMS_EPS = 1e-6
ROPE_THETA = 10000.0
MLA_HEADS = 8
MLA_Q_RANK = 768
MLA_KV_RANK = 256
MLA_NOPE = 128
MLA_ROPE = 64
MLA_V = 128
HY_ORDER = 2
HY_EMB = 33
HY_BANDS = (HY_EMB - 1) // 2
HY_SHORT = 3
HY_FAST_DECAY = 0.3
HY_SLOW_DECAY = 1.5
HY_TARGET = 1e-2
HY_SHIFT = 0.05
GQA_Q_HEADS = 8
GQA_KV_HEADS = 2
GQA_HEAD_DIM = 128
PEER_HEADS = 8
PEER_N_KEYS = 128
PEER_TOPK = 16
PEER_D_KEY = 256

LANES = 128
SUBLANES = 8
VMEM_LIMIT = 56 * 1024 * 1024


def _params(sem):
    return pltpu.CompilerParams(dimension_semantics=sem, vmem_limit_bytes=VMEM_LIMIT)


class Cfg:
    def __init__(self, batch, seq, ctx_len, d):
        self.B, self.L, self.CL, self.D = batch, seq, ctx_len, d
        self.NL = batch * seq
        self.NC = batch * ctx_len
        self.NT = self.NL + self.NC
        self.tm = math.gcd(512, math.gcd(seq, self.NC))


def _linear_kernel(*refs, has_mod, mod_rows, n_row, n_col, n_out, pre, epilogue):
    it = iter(refs)
    x_ref = next(it)
    mod_ref = next(it) if has_mod else None
    w_ref = next(it)
    row_refs = [next(it) for _ in range(n_row)]
    col_refs = [next(it) for _ in range(n_col)]
    out_refs = [next(it) for _ in range(n_out)]
    x = x_ref[...]
    mod = mod_ref[...] if has_mod else None
    if pre is not None:
        x = pre(x)
    if mod_rows is not None:
        sh, sc = mod_rows
        x = x * (1.0 + mod[sc:sc + 1, :]) + mod[sh:sh + 1, :]
    acc = jnp.dot(x.astype(BF16), w_ref[...].astype(BF16), preferred_element_type=F32)
    outs = epilogue(acc, [r[...] for r in row_refs], [c[...] for c in col_refs], mod)
    for o_ref, o in zip(out_refs, outs):
        o_ref[...] = o.astype(o_ref.dtype)


def fused_linear(x, w, *, tm, tn, out_defs, epilogue, seg_fn=None, mod=None, mod_rows=None,
                 w_layer=None, row_extras=(), col_extras=(), pre=None, n_rows=None):
    rows, k = x.shape
    if n_rows is not None:
        rows = n_rows
    n = w.shape[-1]
    grid = (rows // tm, n // tn)
    in_specs = [pl.BlockSpec((tm, k), lambda i, j: (i, 0))]
    args = [x]
    if mod is not None:
        in_specs.append(pl.BlockSpec((None,) + mod.shape[1:], lambda i, j: (seg_fn(i), 0, 0)))
        args.append(mod)
    if w_layer is None:
        in_specs.append(pl.BlockSpec((k, tn), lambda i, j: (0, j)))
    else:
        in_specs.append(pl.BlockSpec((None, k, tn), lambda i, j: (w_layer, 0, j)))
    args.append(w)
    for r in row_extras:
        in_specs.append(pl.BlockSpec((tm, r.shape[1]), lambda i, j: (i, 0)))
        args.append(r)
    for c in col_extras:
        in_specs.append(pl.BlockSpec((c.shape[0], tn), lambda i, j: (0, j)))
        args.append(c)
    out_specs = [pl.BlockSpec((tm, bc), lambda i, j: (i, j)) for (_, bc, _) in out_defs]
    out_shape = [jax.ShapeDtypeStruct((rows, tc), dt) for (tc, _, dt) in out_defs]
    kern = functools.partial(_linear_kernel, has_mod=mod is not None, mod_rows=mod_rows,
                             n_row=len(row_extras), n_col=len(col_extras), n_out=len(out_defs),
                             pre=pre, epilogue=epilogue)
    outs = pl.pallas_call(kern, grid=grid, in_specs=in_specs, out_specs=out_specs, out_shape=out_shape,
                          compiler_params=_params(("parallel", "parallel")))(*args)
    return outs


def _rms(y, gain):
    return y * lax.rsqrt(jnp.mean(jnp.square(y), axis=-1, keepdims=True) + RMS_EPS) * gain


def _rope(y, cos, sin):
    return y * cos + pltpu.roll(y, LANES // 2, 1) * sin


def _layer_norm(v, g, b):
    mu = jnp.mean(v, axis=-1, keepdims=True)
    vc = v - mu
    var = jnp.mean(jnp.square(vc), axis=-1, keepdims=True)
    return vc * lax.rsqrt(var + LN_EPS) * g + b


def residual_ln_linear(cfg, x, y, w, mod, gate_row, lng, lnb, bias=None, n_rows=None):
    d = cfg.D

    def epi(acc, rows, cols, m):
        if bias is not None:
            acc = acc + cols[2]
        v = DEEPNORM_ALPHA * rows[0] + m[gate_row:gate_row + 1, :] * acc
        return [_layer_norm(v, cols[0], cols[1])]

    cols = [lng.reshape(1, d), lnb.reshape(1, d)] + ([bias.reshape(1, d)] if bias is not None else [])
    return fused_linear(y, w, tm=cfg.tm, tn=d, out_defs=[(d, d, F32)], epilogue=epi,
                        seg_fn=_seg_fn(cfg), mod=mod, row_extras=[x], col_extras=cols, n_rows=n_rows)[0]


def _seg_fn(cfg):
    tiles_per_batch = cfg.L // cfg.tm
    nb = cfg.B
    return lambda i: jnp.minimum(i // tiles_per_batch, nb)


ATTN_KEY_CHUNK = 256
ATTN_CHUNK_UNROLL = 8


def _attn_kernel(*refs, n_parts, n_segs, heads, tq):
    q_ref = refs[0]
    pos = 1
    segs = []
    for _ in range(n_segs):
        segs.append((refs[pos:pos + n_parts], refs[pos + n_parts]))
        pos += n_parts + 1
    o_ref, sc_scr, acc_scr = refs[pos], refs[pos + 1], refs[pos + 2]
    width = n_parts * LANES
    qv = q_ref[...]
    q = qv if heads == 1 else jnp.concatenate([qv[:, i * width:(i + 1) * width] for i in range(heads)], 0)
    m = None
    row0 = 0
    for k_refs, _ in segs:
        k = k_refs[0][...] if n_parts == 1 else jnp.concatenate([r[...] for r in k_refs], 1)
        sc = lax.dot_general(k, q, (((1,), (1,)), ((), ())), preferred_element_type=F32)
        sc_scr[pl.ds(row0, sc.shape[0]), :] = sc
        row0 += sc.shape[0]
        ms = jnp.max(sc, axis=0, keepdims=True)
        m = ms if m is None else jnp.maximum(m, ms)
    ones = jnp.ones((2 * SUBLANES, ATTN_KEY_CHUNK), BF16)
    acc_scr[...] = jnp.zeros_like(acc_scr)
    row0 = 0
    for _, vt_ref in segs:
        def chunk(c, carry, vt_ref=vt_ref, row0=row0):
            keys = pl.ds(pl.multiple_of(row0 + c * ATTN_KEY_CHUNK, ATTN_KEY_CHUNK), ATTN_KEY_CHUNK)
            cols = pl.ds(pl.multiple_of(c * ATTN_KEY_CHUNK, ATTN_KEY_CHUNK), ATTN_KEY_CHUNK)
            p = jnp.exp((sc_scr[keys, :] - m).astype(BF16))
            vt = jnp.concatenate([vt_ref[:, cols], ones], 0)
            acc_scr[...] += jnp.dot(vt, p, preferred_element_type=F32)
            return carry

        n_chunks = vt_ref.shape[1] // ATTN_KEY_CHUNK
        lax.fori_loop(0, n_chunks, chunk, 0, unroll=min(n_chunks, ATTN_CHUNK_UNROLL))
        row0 += vt_ref.shape[1]
    acc = acc_scr[...]
    o = (acc[:LANES] / acc[LANES:LANES + 1]).T
    if heads > 1:
        o = jnp.concatenate([o[i * tq:(i + 1) * tq] for i in range(heads)], 1)
    o_ref[...] = o.astype(o_ref.dtype)


def attention(q_arr, q_rows, out_rows, key_segs, *, n_parts, heads, n_batch, n_groups, n_qt, tq, out_rows_total):
    qw = heads * n_parts * LANES
    in_specs = [pl.BlockSpec((tq, qw), lambda b, g, t: (q_rows(b, t), g))]
    args = [q_arr]
    for keys_blk, blk_fn, k_parts, (vt_arr, vt_row) in key_segs:
        for k_arr, k_col in k_parts:
            in_specs.append(pl.BlockSpec((keys_blk, LANES),
                                         lambda b, g, t, blk_fn=blk_fn, k_col=k_col: (blk_fn(b), k_col(g))))
            args.append(k_arr)
        in_specs.append(pl.BlockSpec((LANES, keys_blk),
                                     lambda b, g, t, blk_fn=blk_fn, vt_row=vt_row: (vt_row(g), blk_fn(b))))
        args.append(vt_arr)
    out_spec = pl.BlockSpec((tq, heads * LANES), lambda b, g, t: (out_rows(b, t), g))
    kern = functools.partial(_attn_kernel, n_parts=n_parts, n_segs=len(key_segs), heads=heads, tq=tq)
    total_keys = sum(seg[0] for seg in key_segs)
    return pl.pallas_call(
        kern, grid=(n_batch, n_groups, n_qt), in_specs=in_specs, out_specs=out_spec,
        scratch_shapes=[pltpu.VMEM((total_keys, heads * tq), F32),
                        pltpu.VMEM((LANES + 2 * SUBLANES, heads * tq), F32)],
        out_shape=jax.ShapeDtypeStruct((out_rows_total, n_groups * heads * LANES), BF16),
        compiler_params=_params(("parallel", "parallel", "arbitrary")))(*args)


def _attend_both(cfg, q_arr, k_parts, vt_part, n_parts, heads, n_groups, tq, ctx_out):
    tq = min(tq, cfg.L)
    n_qt = cfg.L // tq
    lat_blocks_ctx = cfg.NL // cfg.CL
    seg_lat = (cfg.L, lambda b: b, k_parts, vt_part)
    seg_ctx = (cfg.CL, lambda b: lat_blocks_ctx + b, k_parts, vt_part)
    lat_rows = lambda b, t: b * n_qt + t
    y_lat = attention(q_arr, lat_rows, lat_rows, [seg_ctx, seg_lat], n_parts=n_parts, heads=heads,
                      n_batch=cfg.B, n_groups=n_groups, n_qt=n_qt, tq=tq, out_rows_total=cfg.NL)
    if not ctx_out:
        return y_lat
    y_ctx = attention(q_arr, lambda b, t: lat_blocks_ctx + b, lambda b, t: b, [seg_ctx], n_parts=n_parts,
                      heads=heads, n_batch=cfg.B, n_groups=n_groups, n_qt=1, tq=cfg.CL, out_rows_total=cfg.NC)
    return jnp.concatenate([y_lat, y_ctx], axis=0)


def _rope_tables(cfg, rot_dim, half_width):
    rows = cfg.L // GRID_W
    row = jnp.repeat(jnp.arange(rows, dtype=F32), GRID_W)
    col = jnp.tile(jnp.arange(GRID_W, dtype=F32), rows)
    quarter = rot_dim // 4
    inv_freq = ROPE_THETA ** (-jnp.arange(quarter, dtype=F32) / quarter)
    ang = jnp.concatenate([row[:, None] * inv_freq, col[:, None] * inv_freq], -1)
    c, s = jnp.cos(ang), jnp.sin(ang)
    pad = half_width - rot_dim // 2
    one = jnp.ones((cfg.L, pad), F32)
    zero = jnp.zeros((cfg.L, pad), F32)
    cos_t = jnp.concatenate([c, one, c, one], -1)
    sin_t = jnp.concatenate([-s, zero, s, zero], -1)
    cos_t = jnp.concatenate([jnp.tile(cos_t, (cfg.B, 1)), jnp.ones((cfg.NC, LANES), F32)], 0)
    sin_t = jnp.concatenate([jnp.tile(sin_t, (cfg.B, 1)), jnp.zeros((cfg.NC, LANES), F32)], 0)
    return cos_t, sin_t


def _deinterleave_cols(w, rot_dim):
    half = rot_dim // 2
    pad = LANES // 2 - half
    ev = w[..., 0::2]
    od = w[..., 1::2]
    z = jnp.zeros(w.shape[:-1] + (pad,), w.dtype)
    return jnp.concatenate([ev, z, od, z], -1)


def mla_layer(cfg, xa, mod, w_dq, q_norm, w_uq, w_dkv, kv_norm, w_ukv, w_o, lng, lnb, ctx_out):
    d = cfg.D
    tm = cfg.tm
    seg = _seg_fn(cfg)
    scale = (MLA_NOPE + MLA_ROPE) ** -0.5
    cos_t, sin_t = _rope_tables(cfg, MLA_ROPE, LANES // 2)
    hq = MLA_HEADS

    w_uq3 = w_uq.reshape(MLA_Q_RANK, hq, MLA_NOPE + MLA_ROPE)
    w_uq_p = jnp.concatenate([w_uq3[..., :MLA_NOPE], _deinterleave_cols(w_uq3[..., MLA_NOPE:], MLA_ROPE)], -1)
    w_uq_p = w_uq_p.reshape(MLA_Q_RANK, hq * 2 * LANES).astype(BF16)
    w_dkv_p = jnp.concatenate([w_dkv[:, :MLA_KV_RANK], _deinterleave_cols(w_dkv[:, MLA_KV_RANK:], MLA_ROPE)],
                              -1).astype(BF16)

    qr, kr = MLA_Q_RANK, MLA_KV_RANK

    def epi_down(acc, rows, cols, m):
        g = cols[0]
        return [_rms(acc[:, :qr], g[:, :qr]),
                _rms(acc[:, qr:qr + kr], g[:, qr:qr + kr]),
                _rope(acc[:, qr + kr:], rows[0], rows[1])]

    w_down = jnp.concatenate([w_dq.astype(BF16), w_dkv_p], -1)
    gains = jnp.concatenate([q_norm, kv_norm, jnp.ones((LANES,), F32)]).reshape(1, -1)
    n_down = qr + kr + LANES
    cq, ckv, kpe = fused_linear(xa, w_down, tm=tm, tn=n_down,
                                out_defs=[(qr, qr, BF16), (kr, kr, BF16), (LANES, LANES, BF16)],
                                epilogue=epi_down, seg_fn=seg, mod=mod, mod_rows=(0, 1),
                                row_extras=[cos_t, sin_t], col_extras=[gains])

    heads_per_tile = 4

    def epi_q(acc, rows, cols, m):
        parts = []
        for hh in range(heads_per_tile):
            c0 = hh * 2 * LANES
            parts.append(acc[:, c0:c0 + LANES] * scale)
            parts.append(_rope(acc[:, c0 + LANES:c0 + 2 * LANES] * scale, rows[0], rows[1]))
        return [jnp.concatenate(parts, -1)]

    tn_q = heads_per_tile * 2 * LANES
    q = fused_linear(cq, w_uq_p, tm=tm, tn=tn_q, out_defs=[(hq * 2 * LANES, tn_q, BF16)],
                     epilogue=epi_q, row_extras=[cos_t, sin_t])[0]

    n_kv = w_ukv.shape[1]
    kv = fused_linear(ckv, w_ukv.astype(BF16), tm=tm, tn=n_kv, out_defs=[(n_kv, n_kv, BF16)],
                      epilogue=lambda acc, rows, cols, m: [acc])[0]

    vt = kv.reshape(cfg.NT, hq, 2, LANES)[:, :, 1, :].reshape(cfg.NT, hq * LANES).T
    y = _attend_both(cfg, q, [(kv, lambda h: 2 * h), (kpe, lambda h: 0)], (vt, lambda h: h),
                     n_parts=2, heads=1, n_groups=hq, tq=512, ctx_out=ctx_out)
    n_rows = None if ctx_out else cfg.NL
    return residual_ln_linear(cfg, xa, y, w_o.astype(BF16), mod, 2, lng, lnb, n_rows=n_rows)


def gqa_layer(cfg, xa, mod, w_qkv, q_norm, k_norm, w_o, lng, lnb, ctx_out):
    tm = cfg.tm
    seg = _seg_fn(cfg)
    hd = GQA_HEAD_DIM
    nq, nk = GQA_Q_HEADS, GQA_KV_HEADS
    scale = hd ** -0.5
    cos_t, sin_t = _rope_tables(cfg, hd, LANES // 2)
    perm = np.concatenate([np.arange(0, hd, 2), np.arange(1, hd, 2)])
    w3 = w_qkv.reshape(cfg.D, nq + 2 * nk, hd)
    w_p = jnp.concatenate([w3[:, :nq + nk, perm], w3[:, nq + nk:, :]], 1).reshape(cfg.D, -1).astype(BF16)
    gains = jnp.stack([q_norm[perm], k_norm[perm]])
    n_chunks = nq + 2 * nk

    def epi(acc, rows, cols, m):
        g = cols[0]
        outs = []
        for c in range(n_chunks):
            y = acc[:, c * hd:(c + 1) * hd]
            if c < nq:
                y = _rope(_rms(y, g[0:1, :hd]) * scale, rows[0], rows[1])
            elif c < nq + nk:
                y = _rope(_rms(y, g[1:2, :hd]), rows[0], rows[1])
            outs.append(y)
        return [jnp.concatenate(outs, -1)]

    n_out = n_chunks * hd
    gains_full = jnp.tile(gains, (1, n_chunks))
    qkv = fused_linear(xa, w_p, tm=tm, tn=n_out, out_defs=[(n_out, n_out, BF16)], epilogue=epi,
                       seg_fn=seg, mod=mod, mod_rows=(0, 1), row_extras=[cos_t, sin_t],
                       col_extras=[gains_full])[0]
    grp = nq // nk
    vt = qkv[:, (nq + nk) * hd:].T
    y = _attend_both(cfg, qkv, [(qkv, lambda g: nq + g)], (vt, lambda g: g),
                     n_parts=1, heads=grp, n_groups=nk, tq=128, ctx_out=ctx_out)
    n_rows = None if ctx_out else cfg.NL
    return residual_ln_linear(cfg, xa, y, w_o.astype(BF16), mod, 2, lng, lnb, n_rows=n_rows)


PEER_E1_PER_STEP = 16
PEER_E1_PER_DOT = 8
GATE_STEP = 128.0
_PAIRS = [(a, b) for a in range(PEER_TOPK) for b in range(PEER_TOPK) if (a + 1) * (b + 1) <= PEER_TOPK]


def _insert_sorted(lst, x):
    out = []
    for a in lst:
        out.append(jnp.maximum(a, x))
        x = jnp.minimum(a, x)
    return out


def _gelu(x):
    return 0.5 * x * (1.0 + lax.erf(x * (2.0 ** -0.5)))


def _peer_route_group(s1_scr, s2_scr, cnt_scr, r2_scr, g):
    nk = PEER_N_KEYS
    k = PEER_TOPK
    neg = jnp.full((PEER_HEADS, LANES), -jnp.inf, F32)

    def key_rows(i):
        return pl.ds(pl.multiple_of(i * PEER_HEADS, PEER_HEADS), PEER_HEADS)

    def top_values(load):
        return lax.fori_loop(0, nk, lambda i, lst: tuple(_insert_sorted(lst, load(i))), (neg,) * k, unroll=4)

    v1 = top_values(lambda i: s1_scr[g, key_rows(i), :])
    v2 = top_values(lambda i: s2_scr[g, key_rows(i), :])
    cand = {ab: v1[ab[0]] + v2[ab[1]] for ab in _PAIRS}
    top = [neg] * k
    for ab in _PAIRS:
        top = _insert_sorted(top, cand[ab])
    tau = top[k - 1]
    cmax = cand[(0, 0)]
    zsum = jnp.zeros_like(tau)
    for ab in _PAIRS:
        zsum = zsum + jnp.where(cand[ab] >= tau, jnp.exp(cand[ab] - cmax), 0.0)
    inv_z = 1.0 / zsum
    m1, m2 = v1[0], v2[0]

    def finish(i, carry):
        rows = key_rows(i)
        x1 = s1_scr[g, rows, :]
        x2 = s2_scr[g, rows, :]
        cnt = jnp.zeros_like(x1)
        r2 = jnp.zeros_like(x2)
        for b in range(k):
            cnt = cnt + jnp.where(x1 + v2[b] >= tau, 1.0, 0.0)
            r2 = r2 + jnp.where(v2[b] > x2, 1.0, 0.0)
        cnt_scr[g, rows, :] = cnt * GATE_STEP
        r2_scr[g, rows, :] = r2 * GATE_STEP
        s1_scr[g, rows, :] = jnp.exp(x1 - m1) * inv_z
        s2_scr[g, rows, :] = jnp.exp(x2 - m2)
        return carry

    lax.fori_loop(0, nk, finish, 0, unroll=2)


def _peer_kernel(x_ref, mod_ref, wq_ref, k1_ref, k2_ref, *rest, tokens):
    n_sub = PEER_E1_PER_STEP // PEER_E1_PER_DOT
    u_refs, vt_refs = rest[:n_sub], rest[n_sub:2 * n_sub]
    (lng_ref, lnb_ref, o_ref, ht_scr, s1_scr, s2_scr, cnt_scr, r2_scr, r2b_scr, p2b_scr, act_scr,
     acc_scr) = rest[2 * n_sub:]
    e = pl.program_id(1)
    n_groups = tokens // LANES
    nh = PEER_HEADS
    nk = PEER_N_KEYS
    pack = 2 * SUBLANES

    @pl.when(e == 0)
    def _route():
        mod = mod_ref[...]
        h = x_ref[...] * (1.0 + mod[4:5, :]) + mod[3:4, :]
        ht = h.T.astype(BF16)
        ht_scr[...] = ht
        qt = jnp.dot(wq_ref[...], ht, preferred_element_type=F32)
        half_rows = nh * (PEER_D_KEY // 2)
        s1 = jnp.dot(k1_ref[...], qt[:half_rows].astype(BF16), preferred_element_type=F32)
        s2 = jnp.dot(k2_ref[...], qt[half_rows:].astype(BF16), preferred_element_type=F32)
        for g in range(n_groups):
            s1_scr[g] = s1[:, g * LANES:(g + 1) * LANES]
            s2_scr[g] = s2[:, g * LANES:(g + 1) * LANES]
        for g in range(n_groups):
            _peer_route_group(s1_scr, s2_scr, cnt_scr, r2_scr, g)
        for g in range(n_groups):
            for hd in range(nh):
                r2b_scr[g, hd] = r2_scr[g, pl.ds(hd, nk, stride=nh), :].astype(BF16)
                p2b_scr[g, hd] = s2_scr[g, pl.ds(hd, nk, stride=nh), :].astype(BF16)
        acc_scr[...] = jnp.zeros_like(acc_scr)

    def up_projection(k):
        act = jnp.dot(u_refs[k][...], ht_scr[...], preferred_element_type=F32)
        for g in range(n_groups):
            act_scr[k % 2, g] = act[:, g * LANES:(g + 1) * LANES]

    up_projection(0)
    for k in range(n_sub):
        if k + 1 < n_sub:
            up_projection(k + 1)
        row0 = pl.multiple_of((e * PEER_E1_PER_STEP + k * PEER_E1_PER_DOT) * nh, nh)
        w_cols = []
        for g in range(n_groups):
            cnt = cnt_scr[g, pl.ds(row0, PEER_E1_PER_DOT * nh), :]
            p1 = s1_scr[g, pl.ds(row0, PEER_E1_PER_DOT * nh), :]
            gates = [jnp.zeros((nk // pack, pack, LANES), BF16) for _ in range(PEER_E1_PER_DOT)]
            for hd in range(nh):
                r2t = r2b_scr[g, hd].reshape(nk // pack, pack, LANES)
                p2t = p2b_scr[g, hd].reshape(nk // pack, pack, LANES)
                for jj in range(PEER_E1_PER_DOT):
                    r = jj * nh + hd
                    cb = jnp.broadcast_to(cnt[r:r + 1, :], (pack, LANES)).astype(BF16)
                    pb = jnp.broadcast_to(p1[r:r + 1, :], (pack, LANES)).astype(BF16)
                    gates[jj] = gates[jj] + jnp.minimum(jnp.maximum(cb[None] - r2t, 0), pb[None]) * p2t
            w_parts = []
            for jj in range(PEER_E1_PER_DOT):
                a = _gelu(act_scr[k % 2, g, pl.ds(jj * nk, nk), :].astype(BF16))
                w_parts.append(gates[jj].reshape(nk, LANES) * a)
            w_cols.append(jnp.concatenate(w_parts, axis=0))
        w = jnp.concatenate(w_cols, axis=1)
        acc_scr[...] += jnp.dot(vt_refs[k][...], w, preferred_element_type=F32)

    @pl.when(e == pl.num_programs(1) - 1)
    def _finish():
        mod = mod_ref[...]
        f = acc_scr[...].T
        v = DEEPNORM_ALPHA * x_ref[...] + mod[5:6, :] * f
        o_ref[...] = _layer_norm(v, lng_ref[...], lnb_ref[...])


def peer_layer(cfg, xa, mod, layer, wq_t, k1_bd, k2_bd, u_b, vt_b, lng, lnb, n_rows):
    d = cfg.D
    t = cfg.tm
    n_exp = u_b.shape[1]
    e_tile = PEER_E1_PER_STEP * PEER_N_KEYS
    seg = _seg_fn(cfg)
    nq = wq_t.shape[1]
    nkh = k1_bd.shape[1]
    ng = t // LANES
    in_specs = [
        pl.BlockSpec((t, d), lambda i, e: (i, 0)),
        pl.BlockSpec((None, 6, d), lambda i, e: (seg(i), 0, 0)),
        pl.BlockSpec((None, nq, d), lambda i, e: (layer, 0, 0), pipeline_mode=pl.Buffered(1)),
        pl.BlockSpec((None, nkh, nkh), lambda i, e: (layer, 0, 0), pipeline_mode=pl.Buffered(1)),
        pl.BlockSpec((None, nkh, nkh), lambda i, e: (layer, 0, 0), pipeline_mode=pl.Buffered(1)),
    ]
    n_sub = PEER_E1_PER_STEP // PEER_E1_PER_DOT
    sub_rows = PEER_E1_PER_DOT * PEER_N_KEYS
    in_specs += [pl.BlockSpec((None, sub_rows, d), lambda i, e, k=k: (layer, e * n_sub + k, 0))
                 for k in range(n_sub)]
    in_specs += [pl.BlockSpec((None, None, d, sub_rows), lambda i, e, k=k: (layer, e * n_sub + k, 0, 0))
                 for k in range(n_sub)]
    in_specs += [
        pl.BlockSpec((1, d), lambda i, e: (0, 0)),
        pl.BlockSpec((1, d), lambda i, e: (0, 0)),
    ]
    scratch = [
        pltpu.VMEM((d, t), BF16),
        pltpu.VMEM((ng, nkh, LANES), F32),
        pltpu.VMEM((ng, nkh, LANES), F32),
        pltpu.VMEM((ng, nkh, LANES), F32),
        pltpu.VMEM((ng, nkh, LANES), F32),
        pltpu.VMEM((ng, PEER_HEADS, PEER_N_KEYS, LANES), BF16),
        pltpu.VMEM((ng, PEER_HEADS, PEER_N_KEYS, LANES), BF16),
        pltpu.VMEM((2, ng, PEER_E1_PER_DOT * PEER_N_KEYS, LANES), F32),
        pltpu.VMEM((d, t), F32),
    ]
    return pl.pallas_call(
        functools.partial(_peer_kernel, tokens=t),
        grid=(n_rows // t, n_exp // e_tile), in_specs=in_specs,
        out_specs=pl.BlockSpec((t, d), lambda i, e: (i, 0)),
        out_shape=jax.ShapeDtypeStruct((n_rows, d), F32), scratch_shapes=scratch,
        compiler_params=_params(("parallel", "arbitrary")))(
            xa, mod, wq_t, k1_bd, k2_bd, *([u_b] * n_sub), *([vt_b] * n_sub), lng.reshape(1, d), lnb.reshape(1, d))


def peer_prepare(peer_w_q, peer_keys1, peer_keys2, peer_u, peer_v):
    depth, d, _ = peer_w_q.shape
    half = PEER_D_KEY // 2
    nh = PEER_HEADS
    wq_t = peer_w_q.reshape(depth, d, nh, 2, half).transpose(0, 3, 2, 4, 1).reshape(depth, 2 * nh * half, d)
    eye = jnp.eye(nh, dtype=F32)

    def bd(keys):
        return jnp.einsum('lhkd,hg->lkhgd', keys, eye).reshape(depth, PEER_N_KEYS * nh, nh * half).astype(BF16)

    sub_rows = PEER_E1_PER_DOT * PEER_N_KEYS
    n_exp = peer_v.shape[1]
    vt = peer_v.astype(BF16).reshape(depth, n_exp // sub_rows, sub_rows, d).transpose(0, 1, 3, 2)
    return (wq_t.astype(BF16), bd(peer_keys1), bd(peer_keys2), peer_u.astype(BF16), vt)


HY_C_TILE = LANES
ROW_PAD = SUBLANES
FFT_UNROLL = 8


def _fft_sizes(length):
    n = 2 * length
    n1 = 64 if n >= 8192 else (32 if n >= 1024 else 16)
    return n1, n // n1


def _fft_mats(n1, n2):
    n = n1 * n2
    two_pi = 2.0 * math.pi
    k2 = jnp.arange(n2, dtype=jnp.int32)
    i1 = jnp.arange(n1, dtype=jnp.int32)
    ang = two_pi * ((k2[:, None] * k2[None, :]) % n2).astype(F32) / n2
    c1 = jnp.concatenate([jnp.cos(ang), -jnp.sin(ang)], 0)
    m = (i1[None, None, :] * k2[:, None, None] + i1[None, None, :] * i1[None, :, None] * n2) % n
    ang = two_pi * m.astype(F32) / n
    gr, gi = jnp.cos(ang), -jnp.sin(ang)
    g = jnp.concatenate([jnp.concatenate([gr, -gi], -1), jnp.concatenate([gi, gr], -1)], -2)
    hr, hi = jnp.swapaxes(gr, 1, 2), -jnp.swapaxes(gi, 1, 2)
    h = jnp.concatenate([jnp.concatenate([hr, -hi], -1), jnp.concatenate([hi, hr], -1)], -2)
    ang = two_pi * ((k2[:n2 // 2, None] * k2[None, :]) % n2).astype(F32) / n2
    c2 = jnp.concatenate([jnp.cos(ang), -jnp.sin(ang)], 1) / n
    return c1.astype(BF16), g.astype(BF16), h.astype(BF16), c2.astype(BF16)


def _fft_stage1(src, c1, a_scr, n1, n2, n2_in):
    pitch = 2 * n2 + ROW_PAD

    def body(i, carry):
        xs = src[pl.ds(i, n2_in, stride=n1), :].astype(BF16)
        a_scr[pl.ds(pl.multiple_of(i * pitch, SUBLANES), 2 * n2), :] = jnp.dot(c1, xs, preferred_element_type=F32)
        return carry

    lax.fori_loop(0, n1, body, 0, unroll=FFT_UNROLL)


def _fft_stage2_load(a_scr, k, n1, n2):
    pitch = 2 * n2 + ROW_PAD
    ar = a_scr[pl.ds(k, n1, stride=pitch), :]
    ai = a_scr[pl.ds(n2 + k, n1, stride=pitch), :]
    return jnp.concatenate([ar, ai], 0).astype(BF16)


def _short_conv(z, w, b):
    rows = z.shape[0]
    idx = lax.broadcasted_iota(jnp.int32, z.shape, 0)
    prev = jnp.where(idx == 0, 0.0, pltpu.roll(z, 1, 0))
    nxt = jnp.where(idx == rows - 1, 0.0, pltpu.roll(z, rows - 1, 0))
    return b + prev * w[0:1, :] + z * w[1:2, :] + nxt * w[2:3, :]


def _hyena_conv_kernel(*refs, first, n1, n2):
    if first:
        (zu_ref, cw_u_ref, cb_u_ref, zx_ref, cw_x_ref, cb_x_ref, bias_ref, sr_ref, si_ref,
         c1_ref, g_ref, h_ref, c2_ref, o_ref, u_scr, y_scr, a_scr, b_scr) = refs
        u_scr[...] = _short_conv(zu_ref[...].astype(F32), cw_u_ref[...], cb_u_ref[...])
    else:
        (zu_ref, zx_ref, cw_x_ref, cb_x_ref, bias_ref, sr_ref, si_ref,
         c1_ref, g_ref, h_ref, c2_ref, o_ref, u_scr, y_scr, a_scr, b_scr) = refs
        u_scr[...] = zu_ref[...]
    _fft_stage1(u_scr, c1_ref[...], a_scr, n1, n2, n2 // 2)
    pitch_b = 2 * n1 + ROW_PAD

    def mid(k, carry):
        x = jnp.dot(g_ref[k], _fft_stage2_load(a_scr, k, n1, n2), preferred_element_type=F32)
        rows = pl.ds(pl.multiple_of(k * n1, n1), n1)
        xr, xi = x[:n1], x[n1:]
        sr, si = sr_ref[rows, :], si_ref[rows, :]
        y = jnp.concatenate([xr * sr - xi * si, xr * si + xi * sr], 0).astype(BF16)
        b_scr[pl.ds(pl.multiple_of(k * pitch_b, SUBLANES), 2 * n1), :] = jnp.dot(
            h_ref[k], y, preferred_element_type=F32)
        return carry

    lax.fori_loop(0, n2, mid, 0, unroll=FFT_UNROLL)

    def last(i, carry):
        br = b_scr[pl.ds(i, n2, stride=pitch_b), :]
        bi = b_scr[pl.ds(n1 + i, n2, stride=pitch_b), :]
        y = jnp.dot(c2_ref[...], jnp.concatenate([br, bi], 0).astype(BF16), preferred_element_type=F32)
        y_scr[pl.ds(i, n2 // 2, stride=n1), :] = y
        return carry

    lax.fori_loop(0, n1, last, 0, unroll=FFT_UNROLL)
    x = _short_conv(zx_ref[...].astype(F32), cw_x_ref[...], cb_x_ref[...])
    u = u_scr[...]
    o_ref[...] = (x * (y_scr[...] + u * bias_ref[...])).astype(o_ref.dtype)


def _hyena_conv(y_prev, z, zu_col0, zx_col0, conv_w, conv_b, bias_row, spec_r, spec_i, spec_col0, mats,
                length, n_seq, row_blk0, out_dtype):
    n1, n2 = _fft_sizes(length)
    c1, g, h, c2 = mats
    c1 = c1[:, :n2 // 2]
    ct = HY_C_TILE
    d = bias_row.shape[1]
    n = 2 * length
    one = pl.Buffered(1)
    first = y_prev is None

    def z_specs(col0):
        return [pl.BlockSpec((length, ct), lambda c, s: (row_blk0 + s, col0 + c)),
                pl.BlockSpec((HY_SHORT, ct), lambda c, s: (0, col0 + c)),
                pl.BlockSpec((1, ct), lambda c, s: (0, col0 + c))]

    if first:
        in_specs = z_specs(zu_col0)
        args = [z, conv_w, conv_b]
    else:
        in_specs = [pl.BlockSpec((length, ct), lambda c, s: (s, c))]
        args = [y_prev]
    in_specs += z_specs(zx_col0) + [
        pl.BlockSpec((1, ct), lambda c, s: (0, c)),
        pl.BlockSpec((n, ct), lambda c, s: (0, spec_col0 + c), pipeline_mode=one),
        pl.BlockSpec((n, ct), lambda c, s: (0, spec_col0 + c), pipeline_mode=one),
        pl.BlockSpec(c1.shape, lambda c, s: (0, 0), pipeline_mode=one),
        pl.BlockSpec(g.shape, lambda c, s: (0, 0, 0), pipeline_mode=one),
        pl.BlockSpec(h.shape, lambda c, s: (0, 0, 0), pipeline_mode=one),
        pl.BlockSpec(c2.shape, lambda c, s: (0, 0), pipeline_mode=one),
    ]
    args += [z, conv_w, conv_b, bias_row, spec_r, spec_i, c1, g, h, c2]
    scratch = [
        pltpu.VMEM((length, ct), F32),
        pltpu.VMEM((length, ct), F32),
        pltpu.VMEM((n1 * (2 * n2 + ROW_PAD), ct), F32),
        pltpu.VMEM((n2 * (2 * n1 + ROW_PAD), ct), F32),
    ]
    return pl.pallas_call(
        functools.partial(_hyena_conv_kernel, first=first, n1=n1, n2=n2),
        grid=(d // ct, n_seq), in_specs=in_specs,
        out_specs=pl.BlockSpec((length, ct), lambda c, s: (s, c)),
        out_shape=jax.ShapeDtypeStruct((n_seq * length, d), out_dtype), scratch_shapes=scratch,
        compiler_params=_params(("parallel", "arbitrary")))(*args)


def _hyena_filter_kernel(emb_ref, w1_ref, b1_ref, fr1_ref, w2_ref, b2_ref, fr2_ref, w3_ref, dl_ref, o_ref):
    hi = lax.Precision.HIGHEST
    emb = emb_ref[...]
    hdn = jnp.sin(fr1_ref[...] * (jnp.dot(emb, w1_ref[...], precision=hi, preferred_element_type=F32) + b1_ref[...]))
    hdn = jnp.sin(fr2_ref[...] * (jnp.dot(hdn, w2_ref[...], precision=hi, preferred_element_type=F32) + b2_ref[...]))
    filt = jnp.dot(hdn, w3_ref[...], precision=hi, preferred_element_type=F32)
    t01 = emb[:, 0:1]
    mask = emb[:, LANES - 1:LANES]
    window = jnp.exp(-t01 * dl_ref[...]) + HY_SHIFT
    o_ref[...] = filt * window * mask


def _hyena_spec_kernel(k_ref, c1_ref, g_ref, sr_ref, si_ref, a_scr, *, n1, n2):
    _fft_stage1(k_ref, c1_ref[...], a_scr, n1, n2, n2)

    def mid(k, carry):
        x = jnp.dot(g_ref[k], _fft_stage2_load(a_scr, k, n1, n2), preferred_element_type=F32)
        rows = pl.ds(pl.multiple_of(k * n1, n1), n1)
        sr_ref[rows, :] = x[:n1]
        si_ref[rows, :] = x[n1:]
        return carry

    lax.fori_loop(0, n2, mid, 0, unroll=FFT_UNROLL)


def _hyena_spectra(length, d, f_w1, f_b1, f_fr1, f_w2, f_b2, f_fr2, f_w3, mats):
    n = 2 * length
    n1, n2 = _fft_sizes(length)
    hid = f_w1.shape[1]
    lag = jnp.concatenate([jnp.arange(length), jnp.zeros((1,), jnp.int32), jnp.arange(length - 1, 0, -1)])
    t01 = jnp.linspace(0.0, 1.0, length, dtype=F32)[:, None]
    w = 2.0 * math.pi * jnp.arange(length, dtype=F32)[:, None] / length
    f = jnp.linspace(1e-4, HY_BANDS - 1, HY_BANDS, dtype=F32)[None, :]
    emb = jnp.concatenate([t01, jnp.cos(f * w), -jnp.sin(f * w)], -1)[lag]
    mask = jnp.ones((n, 1), F32).at[length, 0].set(0.0)
    emb = jnp.concatenate([emb, jnp.zeros((n, LANES - HY_EMB - 1), F32), mask], -1)

    def pad2(a, r, c):
        return jnp.zeros((r, c), F32).at[:a.shape[0], :a.shape[1]].set(a)

    w1 = pad2(f_w1, LANES, LANES)
    b1 = pad2(f_b1[None], 1, LANES)
    fr1 = pad2(f_fr1[None], 1, LANES)
    w2 = pad2(f_w2, LANES, LANES)
    b2 = pad2(f_b2[None], 1, LANES)
    fr2 = pad2(f_fr2[None], 1, LANES)
    w3 = pad2(f_w3, LANES, f_w3.shape[1])
    deltas = jnp.abs(jnp.linspace(math.log(HY_TARGET) / HY_SLOW_DECAY, math.log(HY_TARGET) / HY_FAST_DECAY,
                                  d, dtype=F32))[None, :]
    tr = min(512, length)
    half_tiles = length // tr
    small = lambda shape: pl.BlockSpec(shape, lambda r, o: (0, 0))
    kern = pl.pallas_call(
        _hyena_filter_kernel, grid=(n // tr, HY_ORDER),
        in_specs=[pl.BlockSpec((tr, LANES), lambda r, o: (r, 0)),
                  small((LANES, LANES)), small((1, LANES)), small((1, LANES)),
                  small((LANES, LANES)), small((1, LANES)), small((1, LANES)),
                  pl.BlockSpec((LANES, d), lambda r, o: (0, 2 * o + r // half_tiles)),
                  small((1, d))],
        out_specs=pl.BlockSpec((tr, d), lambda r, o: (r, o)),
        out_shape=jax.ShapeDtypeStruct((n, HY_ORDER * d), F32),
        compiler_params=_params(("parallel", "parallel")))(emb, w1, b1, fr1, w2, b2, fr2, w3, deltas)
    c1, g, _, _ = mats
    ct = HY_C_TILE
    n_cols = HY_ORDER * d
    one = pl.Buffered(1)
    spec_r, spec_i = pl.pallas_call(
        functools.partial(_hyena_spec_kernel, n1=n1, n2=n2), grid=(n_cols // ct,),
        in_specs=[pl.BlockSpec((n, ct), lambda c: (0, c)),
                  pl.BlockSpec(c1.shape, lambda c: (0, 0), pipeline_mode=one),
                  pl.BlockSpec(g.shape, lambda c: (0, 0, 0), pipeline_mode=one)],
        out_specs=[pl.BlockSpec((n, ct), lambda c: (0, c)), pl.BlockSpec((n, ct), lambda c: (0, c))],
        out_shape=[jax.ShapeDtypeStruct((n, n_cols), F32)] * 2,
        scratch_shapes=[pltpu.VMEM((n1 * (2 * n2 + ROW_PAD), ct), F32)],
        compiler_params=_params(("parallel",)))(kern, c1, g)
    return spec_r, spec_i


def hyena_layer(cfg, xa, mod, w_in, b_in, conv_w, conv_b, f_w1, f_b1, f_fr1, f_w2, f_b2, f_fr2, f_w3,
                bias, w_out, b_out, lng, lnb, ctx_out):
    d = cfg.D
    n_ct = d // HY_C_TILE
    z = fused_linear(xa, w_in.astype(BF16), tm=cfg.tm, tn=3 * d // 2, out_defs=[(3 * d, 3 * d // 2, BF16)],
                     epilogue=lambda acc, rows, cols, m: [acc + cols[0]], seg_fn=_seg_fn(cfg), mod=mod,
                     mod_rows=(0, 1), col_extras=[b_in.reshape(1, -1)],
                     n_rows=None if ctx_out else cfg.NL)[0]
    conv_b2 = conv_b.reshape(1, -1)
    seqs = [(cfg.L, cfg.B, 0)]
    if ctx_out:
        seqs.append((cfg.CL, cfg.B, cfg.NL // cfg.CL))
    ys = []
    for length, n_seq, row_blk0 in seqs:
        mats = _fft_mats(*_fft_sizes(length))
        spec_r, spec_i = _hyena_spectra(length, d, f_w1, f_b1, f_fr1, f_w2, f_b2, f_fr2, f_w3, mats)
        y1 = _hyena_conv(None, z, 2 * n_ct, 0, conv_w, conv_b2, bias[0:1], spec_r, spec_i, 0,
                         mats, length, n_seq, row_blk0, F32)
        ys.append(_hyena_conv(y1, z, 0, n_ct, conv_w, conv_b2, bias[1:2], spec_r, spec_i, n_ct,
                              mats, length, n_seq, row_blk0, BF16))
    y = ys[0] if len(ys) == 1 else jnp.concatenate(ys, 0)
    n_rows = None if ctx_out else cfg.NL
    return residual_ln_linear(cfg, xa, y, w_out.astype(BF16), mod, 2, lng, lnb, bias=b_out, n_rows=n_rows)


def ada_modulation(cfg, cvec, ada_w, ada_b, layer):
    d = cfg.D
    out = fused_linear(cvec, ada_w, w_layer=layer, tm=cvec.shape[0], tn=d, out_defs=[(6 * d, d, F32)],
                       epilogue=lambda acc, rows, cols, m: [acc + cols[0]], pre=jax.nn.silu,
                       col_extras=[ada_b[layer].reshape(1, -1)])[0]
    return out[:cfg.B + 1].reshape(cfg.B + 1, 6, d)


def kernel(x, c, ctx, c_ctx, ada_w, ada_b, ln_g, ln_b, mla_w_dq, mla_q_norm, mla_w_uq, mla_w_dkv, mla_kv_norm,
           mla_w_ukv, mla_w_o, hy_w_in, hy_b_in, hy_conv_w, hy_conv_b, hy_f_w1, hy_f_b1, hy_f_fr1, hy_f_w2,
           hy_f_b2, hy_f_fr2, hy_f_w3, hy_bias, hy_w_out, hy_b_out, gqa_w_qkv, gqa_q_norm, gqa_k_norm, gqa_w_o,
           peer_w_q, peer_keys1, peer_keys2, peer_u, peer_v):
    batch, seq, d = x.shape
    cfg = Cfg(batch, seq, ctx.shape[1], d)
    xa = jnp.concatenate([x.reshape(-1, d), ctx.reshape(-1, d)], 0)
    pad_rows = -(batch + 1) % SUBLANES
    cvec = jnp.concatenate([c, c_ctx[None, :], jnp.zeros((pad_rows, d), F32)], 0)
    peer_w = peer_prepare(peer_w_q, peer_keys1, peer_keys2, peer_u, peer_v)
    ia = ib = ic = 0
    for layer in range(DEPTH):
        last = layer == DEPTH - 1
        mod = ada_modulation(cfg, cvec, ada_w, ada_b, layer)
        lng, lnb = ln_g[layer, 0], ln_b[layer, 0]
        kind = layer % N_MIXERS
        if kind == 0:
            xm = mla_layer(cfg, xa, mod, mla_w_dq[ia], mla_q_norm[ia], mla_w_uq[ia], mla_w_dkv[ia],
                           mla_kv_norm[ia], mla_w_ukv[ia], mla_w_o[ia], lng, lnb, not last)
            ia += 1
        elif kind == 1:
            xm = hyena_layer(cfg, xa, mod, hy_w_in[ib], hy_b_in[ib], hy_conv_w[ib], hy_conv_b[ib], hy_f_w1[ib],
                             hy_f_b1[ib], hy_f_fr1[ib], hy_f_w2[ib], hy_f_b2[ib], hy_f_fr2[ib], hy_f_w3[ib],
                             hy_bias[ib], hy_w_out[ib], hy_b_out[ib], lng, lnb, not last)
            ib += 1
        else:
            xm = gqa_layer(cfg, xa, mod, gqa_w_qkv[ic], gqa_q_norm[ic], gqa_k_norm[ic], gqa_w_o[ic],
                           lng, lnb, not last)
            ic += 1
        xa = peer_layer(cfg, xm, mod, layer, *peer_w, ln_g[layer, 1], ln_b[layer, 1], xm.shape[0])
    return xa[:cfg.NL].reshape(batch, seq, d)
```

```python
import functools
import math

import numpy as np
import jax
import jax.numpy as jnp
from jax import lax
from jax.experimental import pallas as pl
from jax.experimental.pallas import tpu as pltpu

F32 = jnp.float32
BF16 = jnp.bfloat16

DEPTH = 4
GRID_W = 64
N_MIXERS = 3
DEEPNORM_ALPHA = (2 * DEPTH) ** 0.25
LN_EPS = 1e-5
RMS_EPS = 1e-6
ROPE_THETA = 10000.0
MLA_HEADS = 8
MLA_Q_RANK = 768
MLA_KV_RANK = 256
MLA_NOPE = 128
MLA_ROPE = 64
MLA_V = 128
HY_ORDER = 2
HY_EMB = 33
HY_BANDS = (HY_EMB - 1) // 2
HY_SHORT = 3
HY_FAST_DECAY = 0.3
HY_SLOW_DECAY = 1.5
HY_TARGET = 1e-2
HY_SHIFT = 0.05
GQA_Q_HEADS = 8
GQA_KV_HEADS = 2
GQA_HEAD_DIM = 128
PEER_HEADS = 8
PEER_N_KEYS = 128
PEER_TOPK = 16
PEER_D_KEY = 256

LANES = 128
SUBLANES = 8
VMEM_LIMIT = 56 * 1024 * 1024


def _params(sem):
    return pltpu.CompilerParams(dimension_semantics=sem, vmem_limit_bytes=VMEM_LIMIT)


class Cfg:
    def __init__(self, batch, seq, ctx_len, d):
        self.B, self.L, self.CL, self.D = batch, seq, ctx_len, d
        self.NL = batch * seq
        self.NC = batch * ctx_len
        self.NT = self.NL + self.NC
        self.tm = math.gcd(512, math.gcd(seq, self.NC))


def _linear_kernel(*refs, has_mod, mod_rows, n_row, n_col, n_out, pre, epilogue):
    it = iter(refs)
    x_ref = next(it)
    mod_ref = next(it) if has_mod else None
    w_ref = next(it)
    row_refs = [next(it) for _ in range(n_row)]
    col_refs = [next(it) for _ in range(n_col)]
    out_refs = [next(it) for _ in range(n_out)]
    x = x_ref[...]
    mod = mod_ref[...] if has_mod else None
    if pre is not None:
        x = pre(x)
    if mod_rows is not None:
        sh, sc = mod_rows
        x = x * (1.0 + mod[sc:sc + 1, :]) + mod[sh:sh + 1, :]
    acc = jnp.dot(x.astype(BF16), w_ref[...].astype(BF16), preferred_element_type=F32)
    outs = epilogue(acc, [r[...] for r in row_refs], [c[...] for c in col_refs], mod)
    for o_ref, o in zip(out_refs, outs):
        o_ref[...] = o.astype(o_ref.dtype)


def fused_linear(x, w, *, tm, tn, out_defs, epilogue, seg_fn=None, mod=None, mod_rows=None,
                 w_layer=None, row_extras=(), col_extras=(), pre=None, n_rows=None):
    rows, k = x.shape
    if n_rows is not None:
        rows = n_rows
    n = w.shape[-1]
    grid = (rows // tm, n // tn)
    in_specs = [pl.BlockSpec((tm, k), lambda i, j: (i, 0))]
    args = [x]
    if mod is not None:
        in_specs.append(pl.BlockSpec((None,) + mod.shape[1:], lambda i, j: (seg_fn(i), 0, 0)))
        args.append(mod)
    if w_layer is None:
        in_specs.append(pl.BlockSpec((k, tn), lambda i, j: (0, j)))
    else:
        in_specs.append(pl.BlockSpec((None, k, tn), lambda i, j: (w_layer, 0, j)))
    args.append(w)
    for r in row_extras:
        in_specs.append(pl.BlockSpec((tm, r.shape[1]), lambda i, j: (i, 0)))
        args.append(r)
    for c in col_extras:
        in_specs.append(pl.BlockSpec((c.shape[0], tn), lambda i, j: (0, j)))
        args.append(c)
    out_specs = [pl.BlockSpec((tm, bc), lambda i, j: (i, j)) for (_, bc, _) in out_defs]
    out_shape = [jax.ShapeDtypeStruct((rows, tc), dt) for (tc, _, dt) in out_defs]
    kern = functools.partial(_linear_kernel, has_mod=mod is not None, mod_rows=mod_rows,
                             n_row=len(row_extras), n_col=len(col_extras), n_out=len(out_defs),
                             pre=pre, epilogue=epilogue)
    outs = pl.pallas_call(kern, grid=grid, in_specs=in_specs, out_specs=out_specs, out_shape=out_shape,
                          compiler_params=_params(("parallel", "parallel")))(*args)
    return outs


def _rms(y, gain):
    return y * lax.rsqrt(jnp.mean(jnp.square(y), axis=-1, keepdims=True) + RMS_EPS) * gain


def _rope(y, cos, sin):
    return y * cos + pltpu.roll(y, LANES // 2, 1) * sin


def _layer_norm(v, g, b):
    mu = jnp.mean(v, axis=-1, keepdims=True)
    vc = v - mu
    var = jnp.mean(jnp.square(vc), axis=-1, keepdims=True)
    return vc * lax.rsqrt(var + LN_EPS) * g + b


def residual_ln_linear(cfg, x, y, w, mod, gate_row, lng, lnb, bias=None, n_rows=None):
    d = cfg.D

    def epi(acc, rows, cols, m):
        if bias is not None:
            acc = acc + cols[2]
        v = DEEPNORM_ALPHA * rows[0] + m[gate_row:gate_row + 1, :] * acc
        return [_layer_norm(v, cols[0], cols[1])]

    cols = [lng.reshape(1, d), lnb.reshape(1, d)] + ([bias.reshape(1, d)] if bias is not None else [])
    return fused_linear(y, w, tm=cfg.tm, tn=d, out_defs=[(d, d, F32)], epilogue=epi,
                        seg_fn=_seg_fn(cfg), mod=mod, row_extras=[x], col_extras=cols, n_rows=n_rows)[0]


def _seg_fn(cfg):
    tiles_per_batch = cfg.L // cfg.tm
    nb = cfg.B
    return lambda i: jnp.minimum(i // tiles_per_batch, nb)


ATTN_KEY_CHUNK = 256
ATTN_CHUNK_UNROLL = 8


def _attn_kernel(*refs, n_parts, n_segs, heads, tq):
    q_ref = refs[0]
    pos = 1
    segs = []
    for _ in range(n_segs):
        segs.append((refs[pos:pos + n_parts], refs[pos + n_parts]))
        pos += n_parts + 1
    o_ref, sc_scr, acc_scr = refs[pos], refs[pos + 1], refs[pos + 2]
    width = n_parts * LANES
    qv = q_ref[...]
    q = qv if heads == 1 else jnp.concatenate([qv[:, i * width:(i + 1) * width] for i in range(heads)], 0)
    m = None
    row0 = 0
    for k_refs, _ in segs:
        k = k_refs[0][...] if n_parts == 1 else jnp.concatenate([r[...] for r in k_refs], 1)
        sc = lax.dot_general(k, q, (((1,), (1,)), ((), ())), preferred_element_type=F32)
        sc_scr[pl.ds(row0, sc.shape[0]), :] = sc
        row0 += sc.shape[0]
        ms = jnp.max(sc, axis=0, keepdims=True)
        m = ms if m is None else jnp.maximum(m, ms)
    ones = jnp.ones((2 * SUBLANES, ATTN_KEY_CHUNK), BF16)
    acc_scr[...] = jnp.zeros_like(acc_scr)
    row0 = 0
    for _, vt_ref in segs:
        def chunk(c, carry, vt_ref=vt_ref, row0=row0):
            keys = pl.ds(pl.multiple_of(row0 + c * ATTN_KEY_CHUNK, ATTN_KEY_CHUNK), ATTN_KEY_CHUNK)
            cols = pl.ds(pl.multiple_of(c * ATTN_KEY_CHUNK, ATTN_KEY_CHUNK), ATTN_KEY_CHUNK)
            p = jnp.exp((sc_scr[keys, :] - m).astype(BF16))
            vt = jnp.concatenate([vt_ref[:, cols], ones], 0)
            acc_scr[...] += jnp.dot(vt, p, preferred_element_type=F32)
            return carry

        n_chunks = vt_ref.shape[1] // ATTN_KEY_CHUNK
        lax.fori_loop(0, n_chunks, chunk, 0, unroll=min(n_chunks, ATTN_CHUNK_UNROLL))
        row0 += vt_ref.shape[1]
    acc = acc_scr[...]
    o = (acc[:LANES] / acc[LANES:LANES + 1]).T
    if heads > 1:
        o = jnp.concatenate([o[i * tq:(i + 1) * tq] for i in range(heads)], 1)
    o_ref[...] = o.astype(o_ref.dtype)


def attention(q_arr, q_rows, out_rows, key_segs, *, n_parts, heads, n_batch, n_groups, n_qt, tq, out_rows_total):
    qw = heads * n_parts * LANES
    in_specs = [pl.BlockSpec((tq, qw), lambda b, g, t: (q_rows(b, t), g))]
    args = [q_arr]
    for keys_blk, blk_fn, k_parts, (vt_arr, vt_row) in key_segs:
        for k_arr, k_col in k_parts:
            in_specs.append(pl.BlockSpec((keys_blk, LANES),
                                         lambda b, g, t, blk_fn=blk_fn, k_col=k_col: (blk_fn(b), k_col(g))))
            args.append(k_arr)
        in_specs.append(pl.BlockSpec((LANES, keys_blk),
                                     lambda b, g, t, blk_fn=blk_fn, vt_row=vt_row: (vt_row(g), blk_fn(b))))
        args.append(vt_arr)
    out_spec = pl.BlockSpec((tq, heads * LANES), lambda b, g, t: (out_rows(b, t), g))
    kern = functools.partial(_attn_kernel, n_parts=n_parts, n_segs=len(key_segs), heads=heads, tq=tq)
    total_keys = sum(seg[0] for seg in key_segs)
    return pl.pallas_call(
        kern, grid=(n_batch, n_groups, n_qt), in_specs=in_specs, out_specs=out_spec,
        scratch_shapes=[pltpu.VMEM((total_keys, heads * tq), F32),
                        pltpu.VMEM((LANES + 2 * SUBLANES, heads * tq), F32)],
        out_shape=jax.ShapeDtypeStruct((out_rows_total, n_groups * heads * LANES), BF16),
        compiler_params=_params(("parallel", "parallel", "arbitrary")))(*args)


def _attend_both(cfg, q_arr, k_parts, vt_part, n_parts, heads, n_groups, tq, ctx_out):
    tq = min(tq, cfg.L)
    n_qt = cfg.L // tq
    lat_blocks_ctx = cfg.NL // cfg.CL
    seg_lat = (cfg.L, lambda b: b, k_parts, vt_part)
    seg_ctx = (cfg.CL, lambda b: lat_blocks_ctx + b, k_parts, vt_part)
    lat_rows = lambda b, t: b * n_qt + t
    y_lat = attention(q_arr, lat_rows, lat_rows, [seg_ctx, seg_lat], n_parts=n_parts, heads=heads,
                      n_batch=cfg.B, n_groups=n_groups, n_qt=n_qt, tq=tq, out_rows_total=cfg.NL)
    if not ctx_out:
        return y_lat
    y_ctx = attention(q_arr, lambda b, t: lat_blocks_ctx + b, lambda b, t: b, [seg_ctx], n_parts=n_parts,
                      heads=heads, n_batch=cfg.B, n_groups=n_groups, n_qt=1, tq=cfg.CL, out_rows_total=cfg.NC)
    return jnp.concatenate([y_lat, y_ctx], axis=0)


def _rope_tables(cfg, rot_dim, half_width):
    rows = cfg.L // GRID_W
    row = jnp.repeat(jnp.arange(rows, dtype=F32), GRID_W)
    col = jnp.tile(jnp.arange(GRID_W, dtype=F32), rows)
    quarter = rot_dim // 4
    inv_freq = ROPE_THETA ** (-jnp.arange(quarter, dtype=F32) / quarter)
    ang = jnp.concatenate([row[:, None] * inv_freq, col[:, None] * inv_freq], -1)
    c, s = jnp.cos(ang), jnp.sin(ang)
    pad = half_width - rot_dim // 2
    one = jnp.ones((cfg.L, pad), F32)
    zero = jnp.zeros((cfg.L, pad), F32)
    cos_t = jnp.concatenate([c, one, c, one], -1)
    sin_t = jnp.concatenate([-s, zero, s, zero], -1)
    cos_t = jnp.concatenate([jnp.tile(cos_t, (cfg.B, 1)), jnp.ones((cfg.NC, LANES), F32)], 0)
    sin_t = jnp.concatenate([jnp.tile(sin_t, (cfg.B, 1)), jnp.zeros((cfg.NC, LANES), F32)], 0)
    return cos_t, sin_t


def _deinterleave_cols(w, rot_dim):
    half = rot_dim // 2
    pad = LANES // 2 - half
    ev = w[..., 0::2]
    od = w[..., 1::2]
    z = jnp.zeros(w.shape[:-1] + (pad,), w.dtype)
    return jnp.concatenate([ev, z, od, z], -1)


def mla_layer(cfg, xa, mod, w_dq, q_norm, w_uq, w_dkv, kv_norm, w_ukv, w_o, lng, lnb, ctx_out):
    d = cfg.D
    tm = cfg.tm
    seg = _seg_fn(cfg)
    scale = (MLA_NOPE + MLA_ROPE) ** -0.5
    cos_t, sin_t = _rope_tables(cfg, MLA_ROPE, LANES // 2)
    hq = MLA_HEADS

    w_uq3 = w_uq.reshape(MLA_Q_RANK, hq, MLA_NOPE + MLA_ROPE)
    w_uq_p = jnp.concatenate([w_uq3[..., :MLA_NOPE], _deinterleave_cols(w_uq3[..., MLA_NOPE:], MLA_ROPE)], -1)
    w_uq_p = w_uq_p.reshape(MLA_Q_RANK, hq * 2 * LANES).astype(BF16)
    w_dkv_p = jnp.concatenate([w_dkv[:, :MLA_KV_RANK], _deinterleave_cols(w_dkv[:, MLA_KV_RANK:], MLA_ROPE)],
                              -1).astype(BF16)

    qr, kr = MLA_Q_RANK, MLA_KV_RANK

    def epi_down(acc, rows, cols, m):
        g = cols[0]
        return [_rms(acc[:, :qr], g[:, :qr]),
                _rms(acc[:, qr:qr + kr], g[:, qr:qr + kr]),
                _rope(acc[:, qr + kr:], rows[0], rows[1])]

    w_down = jnp.concatenate([w_dq.astype(BF16), w_dkv_p], -1)
    gains = jnp.concatenate([q_norm, kv_norm, jnp.ones((LANES,), F32)]).reshape(1, -1)
    n_down = qr + kr + LANES
    cq, ckv, kpe = fused_linear(xa, w_down, tm=tm, tn=n_down,
                                out_defs=[(qr, qr, BF16), (kr, kr, BF16), (LANES, LANES, BF16)],
                                epilogue=epi_down, seg_fn=seg, mod=mod, mod_rows=(0, 1),
                                row_extras=[cos_t, sin_t], col_extras=[gains])

    heads_per_tile = 4

    def epi_q(acc, rows, cols, m):
        parts = []
        for hh in range(heads_per_tile):
            c0 = hh * 2 * LANES
            parts.append(acc[:, c0:c0 + LANES] * scale)
            parts.append(_rope(acc[:, c0 + LANES:c0 + 2 * LANES] * scale, rows[0], rows[1]))
        return [jnp.concatenate(parts, -1)]

    tn_q = heads_per_tile * 2 * LANES
    q = fused_linear(cq, w_uq_p, tm=tm, tn=tn_q, out_defs=[(hq * 2 * LANES, tn_q, BF16)],
                     epilogue=epi_q, row_extras=[cos_t, sin_t])[0]

    n_kv = w_ukv.shape[1]
    kv = fused_linear(ckv, w_ukv.astype(BF16), tm=tm, tn=n_kv, out_defs=[(n_kv, n_kv, BF16)],
                      epilogue=lambda acc, rows, cols, m: [acc])[0]

    vt = kv.reshape(cfg.NT, hq, 2, LANES)[:, :, 1, :].reshape(cfg.NT, hq * LANES).T
    y = _attend_both(cfg, q, [(kv, lambda h: 2 * h), (kpe, lambda h: 0)], (vt, lambda h: h),
                     n_parts=2, heads=1, n_groups=hq, tq=512, ctx_out=ctx_out)
    n_rows = None if ctx_out else cfg.NL
    return residual_ln_linear(cfg, xa, y, w_o.astype(BF16), mod, 2, lng, lnb, n_rows=n_rows)


def gqa_layer(cfg, xa, mod, w_qkv, q_norm, k_norm, w_o, lng, lnb, ctx_out):
    tm = cfg.tm
    seg = _seg_fn(cfg)
    hd = GQA_HEAD_DIM
    nq, nk = GQA_Q_HEADS, GQA_KV_HEADS
    scale = hd ** -0.5
    cos_t, sin_t = _rope_tables(cfg, hd, LANES // 2)
    perm = np.concatenate([np.arange(0, hd, 2), np.arange(1, hd, 2)])
    w3 = w_qkv.reshape(cfg.D, nq + 2 * nk, hd)
    w_p = jnp.concatenate([w3[:, :nq + nk, perm], w3[:, nq + nk:, :]], 1).reshape(cfg.D, -1).astype(BF16)
    gains = jnp.stack([q_norm[perm], k_norm[perm]])
    n_chunks = nq + 2 * nk

    def epi(acc, rows, cols, m):
        g = cols[0]
        outs = []
        for c in range(n_chunks):
            y = acc[:, c * hd:(c + 1) * hd]
            if c < nq:
                y = _rope(_rms(y, g[0:1, :hd]) * scale, rows[0], rows[1])
            elif c < nq + nk:
                y = _rope(_rms(y, g[1:2, :hd]), rows[0], rows[1])
            outs.append(y)
        return [jnp.concatenate(outs, -1)]

    n_out = n_chunks * hd
    gains_full = jnp.tile(gains, (1, n_chunks))
    qkv = fused_linear(xa, w_p, tm=tm, tn=n_out, out_defs=[(n_out, n_out, BF16)], epilogue=epi,
                       seg_fn=seg, mod=mod, mod_rows=(0, 1), row_extras=[cos_t, sin_t],
                       col_extras=[gains_full])[0]
    grp = nq // nk
    vt = qkv[:, (nq + nk) * hd:].T
    y = _attend_both(cfg, qkv, [(qkv, lambda g: nq + g)], (vt, lambda g: g),
                     n_parts=1, heads=grp, n_groups=nk, tq=128, ctx_out=ctx_out)
    n_rows = None if ctx_out else cfg.NL
    return residual_ln_linear(cfg, xa, y, w_o.astype(BF16), mod, 2, lng, lnb, n_rows=n_rows)


PEER_E1_PER_STEP = 16
PEER_E1_PER_DOT = 8
GATE_STEP = 128.0
_PAIRS = [(a, b) for a in range(PEER_TOPK) for b in range(PEER_TOPK) if (a + 1) * (b + 1) <= PEER_TOPK]


def _insert_sorted(lst, x):
    out = []
    for a in lst:
        out.append(jnp.maximum(a, x))
        x = jnp.minimum(a, x)
    return out


def _gelu(x):
    return 0.5 * x * (1.0 + lax.erf(x * (2.0 ** -0.5)))


def _peer_route_group(s1_scr, s2_scr, cnt_scr, r2_scr, g):
    nk = PEER_N_KEYS
    k = PEER_TOPK
    neg = jnp.full((PEER_HEADS, LANES), -jnp.inf, F32)

    def key_rows(i):
        return pl.ds(pl.multiple_of(i * PEER_HEADS, PEER_HEADS), PEER_HEADS)

    def top_values(load):
        return lax.fori_loop(0, nk, lambda i, lst: tuple(_insert_sorted(lst, load(i))), (neg,) * k, unroll=4)

    v1 = top_values(lambda i: s1_scr[g, key_rows(i), :])
    v2 = top_values(lambda i: s2_scr[g, key_rows(i), :])
    cand = {ab: v1[ab[0]] + v2[ab[1]] for ab in _PAIRS}
    top = [neg] * k
    for ab in _PAIRS:
        top = _insert_sorted(top, cand[ab])
    tau = top[k - 1]
    cmax = cand[(0, 0)]
    zsum = jnp.zeros_like(tau)
    for ab in _PAIRS:
        zsum = zsum + jnp.where(cand[ab] >= tau, jnp.exp(cand[ab] - cmax), 0.0)
    inv_z = 1.0 / zsum
    m1, m2 = v1[0], v2[0]

    def finish(i, carry):
        rows = key_rows(i)
        x1 = s1_scr[g, rows, :]
        x2 = s2_scr[g, rows, :]
        cnt = jnp.zeros_like(x1)
        r2 = jnp.zeros_like(x2)
        for b in range(k):
            cnt = cnt + jnp.where(x1 + v2[b] >= tau, 1.0, 0.0)
            r2 = r2 + jnp.where(v2[b] > x2, 1.0, 0.0)
        cnt_scr[g, rows, :] = cnt * GATE_STEP
        r2_scr[g, rows, :] = r2 * GATE_STEP
        s1_scr[g, rows, :] = jnp.exp(x1 - m1) * inv_z
        s2_scr[g, rows, :] = jnp.exp(x2 - m2)
        return carry

    lax.fori_loop(0, nk, finish, 0, unroll=2)


def _peer_kernel(x_ref, mod_ref, wq_ref, k1_ref, k2_ref, *rest, tokens):
    n_sub = PEER_E1_PER_STEP // PEER_E1_PER_DOT
    u_refs, vt_refs = rest[:n_sub], rest[n_sub:2 * n_sub]
    (lng_ref, lnb_ref, o_ref, ht_scr, s1_scr, s2_scr, cnt_scr, r2_scr, r2b_scr, p2b_scr, act_scr,
     acc_scr) = rest[2 * n_sub:]
    e = pl.program_id(1)
    n_groups = tokens // LANES
    nh = PEER_HEADS
    nk = PEER_N_KEYS
    pack = 2 * SUBLANES

    @pl.when(e == 0)
    def _route():
        mod = mod_ref[...]
        h = x_ref[...] * (1.0 + mod[4:5, :]) + mod[3:4, :]
        ht = h.T.astype(BF16)
        ht_scr[...] = ht
        qt = jnp.dot(wq_ref[...], ht, preferred_element_type=F32)
        half_rows = nh * (PEER_D_KEY // 2)
        s1 = jnp.dot(k1_ref[...], qt[:half_rows].astype(BF16), preferred_element_type=F32)
        s2 = jnp.dot(k2_ref[...], qt[half_rows:].astype(BF16), preferred_element_type=F32)
        for g in range(n_groups):
            s1_scr[g] = s1[:, g * LANES:(g + 1) * LANES]
            s2_scr[g] = s2[:, g * LANES:(g + 1) * LANES]
        for g in range(n_groups):
            _peer_route_group(s1_scr, s2_scr, cnt_scr, r2_scr, g)
        for g in range(n_groups):
            for hd in range(nh):
                r2b_scr[g, hd] = r2_scr[g, pl.ds(hd, nk, stride=nh), :].astype(BF16)
                p2b_scr[g, hd] = s2_scr[g, pl.ds(hd, nk, stride=nh), :].astype(BF16)
        acc_scr[...] = jnp.zeros_like(acc_scr)

    def up_projection(k):
        act = jnp.dot(u_refs[k][...], ht_scr[...], preferred_element_type=F32)
        for g in range(n_groups):
            act_scr[k % 2, g] = act[:, g * LANES:(g + 1) * LANES]

    up_projection(0)
    for k in range(n_sub):
        if k + 1 < n_sub:
            up_projection(k + 1)
        row0 = pl.multiple_of((e * PEER_E1_PER_STEP + k * PEER_E1_PER_DOT) * nh, nh)
        w_cols = []
        for g in range(n_groups):
            cnt = cnt_scr[g, pl.ds(row0, PEER_E1_PER_DOT * nh), :]
            p1 = s1_scr[g, pl.ds(row0, PEER_E1_PER_DOT * nh), :]
            gates = [jnp.zeros((nk // pack, pack, LANES), BF16) for _ in range(PEER_E1_PER_DOT)]
            for hd in range(nh):
                r2t = r2b_scr[g, hd].reshape(nk // pack, pack, LANES)
                p2t = p2b_scr[g, hd].reshape(nk // pack, pack, LANES)
                for jj in range(PEER_E1_PER_DOT):
                    r = jj * nh + hd
                    cb = jnp.broadcast_to(cnt[r:r + 1, :], (pack, LANES)).astype(BF16)
                    pb = jnp.broadcast_to(p1[r:r + 1, :], (pack, LANES)).astype(BF16)
                    gates[jj] = gates[jj] + jnp.minimum(jnp.maximum(cb[None] - r2t, 0), pb[None]) * p2t
            w_parts = []
            for jj in range(PEER_E1_PER_DOT):
                a = _gelu(act_scr[k % 2, g, pl.ds(jj * nk, nk), :].astype(BF16))
                w_parts.append(gates[jj].reshape(nk, LANES) * a)
            w_cols.append(jnp.concatenate(w_parts, axis=0))
        w = jnp.concatenate(w_cols, axis=1)
        acc_scr[...] += lax.dot_general(vt_refs[k][...], w, (((0,), (0,)), ((), ())),
                                        preferred_element_type=F32)

    @pl.when(e == pl.num_programs(1) - 1)
    def _finish():
        mod = mod_ref[...]
        f = acc_scr[...].T
        v = DEEPNORM_ALPHA * x_ref[...] + mod[5:6, :] * f
        o_ref[...] = _layer_norm(v, lng_ref[...], lnb_ref[...])


def peer_layer(cfg, xa, mod, layer, wq_t, k1_bd, k2_bd, u_b, vt_b, lng, lnb, n_rows):
    d = cfg.D
    t = cfg.tm
    n_exp = u_b.shape[1]
    e_tile = PEER_E1_PER_STEP * PEER_N_KEYS
    seg = _seg_fn(cfg)
    nq = wq_t.shape[1]
    nkh = k1_bd.shape[1]
    ng = t // LANES
    in_specs = [
        pl.BlockSpec((t, d), lambda i, e: (i, 0)),
        pl.BlockSpec((None, 6, d), lambda i, e: (seg(i), 0, 0)),
        pl.BlockSpec((None, nq, d), lambda i, e: (layer, 0, 0), pipeline_mode=pl.Buffered(1)),
        pl.BlockSpec((None, nkh, nkh), lambda i, e: (layer, 0, 0), pipeline_mode=pl.Buffered(1)),
        pl.BlockSpec((None, nkh, nkh), lambda i, e: (layer, 0, 0), pipeline_mode=pl.Buffered(1)),
    ]
    n_sub = PEER_E1_PER_STEP // PEER_E1_PER_DOT
    sub_rows = PEER_E1_PER_DOT * PEER_N_KEYS
    in_specs += [pl.BlockSpec((None, sub_rows, d), lambda i, e, k=k: (layer, e * n_sub + k, 0))
                 for k in range(n_sub)]
    in_specs += [pl.BlockSpec((None, sub_rows, d), lambda i, e, k=k: (layer, e * n_sub + k, 0))
                 for k in range(n_sub)]
    in_specs += [
        pl.BlockSpec((1, d), lambda i, e: (0, 0)),
        pl.BlockSpec((1, d), lambda i, e: (0, 0)),
    ]
    scratch = [
        pltpu.VMEM((d, t), BF16),
        pltpu.VMEM((ng, nkh, LANES), F32),
        pltpu.VMEM((ng, nkh, LANES), F32),
        pltpu.VMEM((ng, nkh, LANES), F32),
        pltpu.VMEM((ng, nkh, LANES), F32),
        pltpu.VMEM((ng, PEER_HEADS, PEER_N_KEYS, LANES), BF16),
        pltpu.VMEM((ng, PEER_HEADS, PEER_N_KEYS, LANES), BF16),
        pltpu.VMEM((2, ng, PEER_E1_PER_DOT * PEER_N_KEYS, LANES), F32),
        pltpu.VMEM((d, t), F32),
    ]
    return pl.pallas_call(
        functools.partial(_peer_kernel, tokens=t),
        grid=(n_rows // t, n_exp // e_tile), in_specs=in_specs,
        out_specs=pl.BlockSpec((t, d), lambda i, e: (i, 0)),
        out_shape=jax.ShapeDtypeStruct((n_rows, d), F32), scratch_shapes=scratch,
        compiler_params=_params(("parallel", "arbitrary")))(
            xa, mod, wq_t, k1_bd, k2_bd, *([u_b] * n_sub), *([vt_b] * n_sub), lng.reshape(1, d), lnb.reshape(1, d))


def peer_prepare(peer_w_q, peer_keys1, peer_keys2, peer_u, peer_v):
    depth, d, _ = peer_w_q.shape
    half = PEER_D_KEY // 2
    nh = PEER_HEADS
    wq_t = peer_w_q.reshape(depth, d, nh, 2, half).transpose(0, 3, 2, 4, 1).reshape(depth, 2 * nh * half, d)
    eye = jnp.eye(nh, dtype=F32)

    def bd(keys):
        return jnp.einsum('lhkd,hg->lkhgd', keys, eye).reshape(depth, PEER_N_KEYS * nh, nh * half).astype(BF16)

    return (wq_t.astype(BF16), bd(peer_keys1), bd(peer_keys2), peer_u.astype(BF16), peer_v.astype(BF16))


HY_C_TILE = LANES
ROW_PAD = SUBLANES
FFT_UNROLL = 8


def _fft_sizes(length):
    n = 2 * length
    n1 = 64 if n >= 8192 else (32 if n >= 1024 else 16)
    return n1, n // n1


def _fft_mats(n1, n2):
    n = n1 * n2
    two_pi = 2.0 * math.pi
    k2 = jnp.arange(n2, dtype=jnp.int32)
    i1 = jnp.arange(n1, dtype=jnp.int32)
    ang = two_pi * ((k2[:, None] * k2[None, :]) % n2).astype(F32) / n2
    c1 = jnp.concatenate([jnp.cos(ang), -jnp.sin(ang)], 0)
    m = (i1[None, None, :] * k2[:, None, None] + i1[None, None, :] * i1[None, :, None] * n2) % n
    ang = two_pi * m.astype(F32) / n
    gr, gi = jnp.cos(ang), -jnp.sin(ang)
    g = jnp.concatenate([jnp.concatenate([gr, -gi], -1), jnp.concatenate([gi, gr], -1)], -2)
    hr, hi = jnp.swapaxes(gr, 1, 2), -jnp.swapaxes(gi, 1, 2)
    h = jnp.concatenate([jnp.concatenate([hr, -hi], -1), jnp.concatenate([hi, hr], -1)], -2)
    ang = two_pi * ((k2[:n2 // 2, None] * k2[None, :]) % n2).astype(F32) / n2
    c2 = jnp.concatenate([jnp.cos(ang), -jnp.sin(ang)], 1) / n
    return c1.astype(BF16), g.astype(BF16), h.astype(BF16), c2.astype(BF16)


def _fft_stage1(src, c1, a_scr, n1, n2, n2_in):
    pitch = 2 * n2 + ROW_PAD

    def body(i, carry):
        xs = src[pl.ds(i, n2_in, stride=n1), :].astype(BF16)
        a_scr[pl.ds(pl.multiple_of(i * pitch, SUBLANES), 2 * n2), :] = jnp.dot(c1, xs, preferred_element_type=F32)
        return carry

    lax.fori_loop(0, n1, body, 0, unroll=FFT_UNROLL)


def _fft_stage2_load(a_scr, k, n1, n2):
    pitch = 2 * n2 + ROW_PAD
    ar = a_scr[pl.ds(k, n1, stride=pitch), :]
    ai = a_scr[pl.ds(n2 + k, n1, stride=pitch), :]
    return jnp.concatenate([ar, ai], 0).astype(BF16)


def _short_conv(z, w, b):
    rows = z.shape[0]
    idx = lax.broadcasted_iota(jnp.int32, z.shape, 0)
    prev = jnp.where(idx == 0, 0.0, pltpu.roll(z, 1, 0))
    nxt = jnp.where(idx == rows - 1, 0.0, pltpu.roll(z, rows - 1, 0))
    return b + prev * w[0:1, :] + z * w[1:2, :] + nxt * w[2:3, :]


def _hyena_conv_kernel(*refs, first, n1, n2):
    if first:
        (zu_ref, cw_u_ref, cb_u_ref, zx_ref, cw_x_ref, cb_x_ref, bias_ref, sr_ref, si_ref,
         c1_ref, g_ref, h_ref, c2_ref, o_ref, u_scr, y_scr, a_scr, b_scr) = refs
        u_scr[...] = _short_conv(zu_ref[...].astype(F32), cw_u_ref[...], cb_u_ref[...])
    else:
        (zu_ref, zx_ref, cw_x_ref, cb_x_ref, bias_ref, sr_ref, si_ref,
         c1_ref, g_ref, h_ref, c2_ref, o_ref, u_scr, y_scr, a_scr, b_scr) = refs
        u_scr[...] = zu_ref[...]
    _fft_stage1(u_scr, c1_ref[...], a_scr, n1, n2, n2 // 2)
    pitch_b = 2 * n1 + ROW_PAD

    def mid(k, carry):
        x = jnp.dot(g_ref[k], _fft_stage2_load(a_scr, k, n1, n2), preferred_element_type=F32)
        rows = pl.ds(pl.multiple_of(k * n1, n1), n1)
        xr, xi = x[:n1], x[n1:]
        sr, si = sr_ref[rows, :], si_ref[rows, :]
        y = jnp.concatenate([xr * sr - xi * si, xr * si + xi * sr], 0).astype(BF16)
        b_scr[pl.ds(pl.multiple_of(k * pitch_b, SUBLANES), 2 * n1), :] = jnp.dot(
            h_ref[k], y, preferred_element_type=F32)
        return carry

    lax.fori_loop(0, n2, mid, 0, unroll=FFT_UNROLL)

    def last(i, carry):
        br = b_scr[pl.ds(i, n2, stride=pitch_b), :]
        bi = b_scr[pl.ds(n1 + i, n2, stride=pitch_b), :]
        y = jnp.dot(c2_ref[...], jnp.concatenate([br, bi], 0).astype(BF16), preferred_element_type=F32)
        y_scr[pl.ds(i, n2 // 2, stride=n1), :] = y
        return carry

    lax.fori_loop(0, n1, last, 0, unroll=FFT_UNROLL)
    x = _short_conv(zx_ref[...].astype(F32), cw_x_ref[...], cb_x_ref[...])
    u = u_scr[...]
    o_ref[...] = (x * (y_scr[...] + u * bias_ref[...])).astype(o_ref.dtype)


def _hyena_conv(y_prev, z, zu_col0, zx_col0, conv_w, conv_b, bias_row, spec_r, spec_i, spec_col0, mats,
                length, n_seq, row_blk0, out_dtype):
    n1, n2 = _fft_sizes(length)
    c1, g, h, c2 = mats
    c1 = c1[:, :n2 // 2]
    ct = HY_C_TILE
    d = bias_row.shape[1]
    n = 2 * length
    one = pl.Buffered(1)
    first = y_prev is None

    def z_specs(col0):
        return [pl.BlockSpec((length, ct), lambda c, s: (row_blk0 + s, col0 + c)),
                pl.BlockSpec((HY_SHORT, ct), lambda c, s: (0, col0 + c)),
                pl.BlockSpec((1, ct), lambda c, s: (0, col0 + c))]

    if first:
        in_specs = z_specs(zu_col0)
        args = [z, conv_w, conv_b]
    else:
        in_specs = [pl.BlockSpec((length, ct), lambda c, s: (s, c))]
        args = [y_prev]
    in_specs += z_specs(zx_col0) + [
        pl.BlockSpec((1, ct), lambda c, s: (0, c)),
        pl.BlockSpec((n, ct), lambda c, s: (0, spec_col0 + c), pipeline_mode=one),
        pl.BlockSpec((n, ct), lambda c, s: (0, spec_col0 + c), pipeline_mode=one),
        pl.BlockSpec(c1.shape, lambda c, s: (0, 0), pipeline_mode=one),
        pl.BlockSpec(g.shape, lambda c, s: (0, 0, 0), pipeline_mode=one),
        pl.BlockSpec(h.shape, lambda c, s: (0, 0, 0), pipeline_mode=one),
        pl.BlockSpec(c2.shape, lambda c, s: (0, 0), pipeline_mode=one),
    ]
    args += [z, conv_w, conv_b, bias_row, spec_r, spec_i, c1, g, h, c2]
    scratch = [
        pltpu.VMEM((length, ct), F32),
        pltpu.VMEM((length, ct), F32),
        pltpu.VMEM((n1 * (2 * n2 + ROW_PAD), ct), F32),
        pltpu.VMEM((n2 * (2 * n1 + ROW_PAD), ct), F32),
    ]
    return pl.pallas_call(
        functools.partial(_hyena_conv_kernel, first=first, n1=n1, n2=n2),
        grid=(d // ct, n_seq), in_specs=in_specs,
        out_specs=pl.BlockSpec((length, ct), lambda c, s: (s, c)),
        out_shape=jax.ShapeDtypeStruct((n_seq * length, d), out_dtype), scratch_shapes=scratch,
        compiler_params=_params(("parallel", "arbitrary")))(*args)


def _hyena_filter_kernel(emb_ref, w1_ref, b1_ref, fr1_ref, w2_ref, b2_ref, fr2_ref, w3_ref, dl_ref, o_ref):
    hi = lax.Precision.HIGHEST
    emb = emb_ref[...]
    hdn = jnp.sin(fr1_ref[...] * (jnp.dot(emb, w1_ref[...], precision=hi, preferred_element_type=F32) + b1_ref[...]))
    hdn = jnp.sin(fr2_ref[...] * (jnp.dot(hdn, w2_ref[...], precision=hi, preferred_element_type=F32) + b2_ref[...]))
    filt = jnp.dot(hdn, w3_ref[...], precision=hi, preferred_element_type=F32)
    t01 = emb[:, 0:1]
    mask = emb[:, LANES - 1:LANES]
    window = jnp.exp(-t01 * dl_ref[...]) + HY_SHIFT
    o_ref[...] = filt * window * mask


def _hyena_spec_kernel(k_ref, c1_ref, g_ref, sr_ref, si_ref, a_scr, *, n1, n2):
    _fft_stage1(k_ref, c1_ref[...], a_scr, n1, n2, n2)

    def mid(k, carry):
        x = jnp.dot(g_ref[k], _fft_stage2_load(a_scr, k, n1, n2), preferred_element_type=F32)
        rows = pl.ds(pl.multiple_of(k * n1, n1), n1)
        sr_ref[rows, :] = x[:n1]
        si_ref[rows, :] = x[n1:]
        return carry

    lax.fori_loop(0, n2, mid, 0, unroll=FFT_UNROLL)


def _hyena_spectra(length, d, f_w1, f_b1, f_fr1, f_w2, f_b2, f_fr2, f_w3, mats):
    n = 2 * length
    n1, n2 = _fft_sizes(length)
    hid = f_w1.shape[1]
    lag = jnp.concatenate([jnp.arange(length), jnp.zeros((1,), jnp.int32), jnp.arange(length - 1, 0, -1)])
    t01 = jnp.linspace(0.0, 1.0, length, dtype=F32)[:, None]
    w = 2.0 * math.pi * jnp.arange(length, dtype=F32)[:, None] / length
    f = jnp.linspace(1e-4, HY_BANDS - 1, HY_BANDS, dtype=F32)[None, :]
    emb = jnp.concatenate([t01, jnp.cos(f * w), -jnp.sin(f * w)], -1)[lag]
    mask = jnp.ones((n, 1), F32).at[length, 0].set(0.0)
    emb = jnp.concatenate([emb, jnp.zeros((n, LANES - HY_EMB - 1), F32), mask], -1)

    def pad2(a, r, c):
        return jnp.zeros((r, c), F32).at[:a.shape[0], :a.shape[1]].set(a)

    w1 = pad2(f_w1, LANES, LANES)
    b1 = pad2(f_b1[None], 1, LANES)
    fr1 = pad2(f_fr1[None], 1, LANES)
    w2 = pad2(f_w2, LANES, LANES)
    b2 = pad2(f_b2[None], 1, LANES)
    fr2 = pad2(f_fr2[None], 1, LANES)
    w3 = pad2(f_w3, LANES, f_w3.shape[1])
    deltas = jnp.abs(jnp.linspace(math.log(HY_TARGET) / HY_SLOW_DECAY, math.log(HY_TARGET) / HY_FAST_DECAY,
                                  d, dtype=F32))[None, :]
    tr = min(512, length)
    half_tiles = length // tr
    small = lambda shape: pl.BlockSpec(shape, lambda r, o: (0, 0))
    kern = pl.pallas_call(
        _hyena_filter_kernel, grid=(n // tr, HY_ORDER),
        in_specs=[pl.BlockSpec((tr, LANES), lambda r, o: (r, 0)),
                  small((LANES, LANES)), small((1, LANES)), small((1, LANES)),
                  small((LANES, LANES)), small((1, LANES)), small((1, LANES)),
                  pl.BlockSpec((LANES, d), lambda r, o: (0, 2 * o + r // half_tiles)),
                  small((1, d))],
        out_specs=pl.BlockSpec((tr, d), lambda r, o: (r, o)),
        out_shape=jax.ShapeDtypeStruct((n, HY_ORDER * d), F32),
        compiler_params=_params(("parallel", "parallel")))(emb, w1, b1, fr1, w2, b2, fr2, w3, deltas)
    c1, g, _, _ = mats
    ct = HY_C_TILE
    n_cols = HY_ORDER * d
    one = pl.Buffered(1)
    spec_r, spec_i = pl.pallas_call(
        functools.partial(_hyena_spec_kernel, n1=n1, n2=n2), grid=(n_cols // ct,),
        in_specs=[pl.BlockSpec((n, ct), lambda c: (0, c)),
                  pl.BlockSpec(c1.shape, lambda c: (0, 0), pipeline_mode=one),
                  pl.BlockSpec(g.shape, lambda c: (0, 0, 0), pipeline_mode=one)],
        out_specs=[pl.BlockSpec((n, ct), lambda c: (0, c)), pl.BlockSpec((n, ct), lambda c: (0, c))],
        out_shape=[jax.ShapeDtypeStruct((n, n_cols), F32)] * 2,
        scratch_shapes=[pltpu.VMEM((n1 * (2 * n2 + ROW_PAD), ct), F32)],
        compiler_params=_params(("parallel",)))(kern, c1, g)
    return spec_r, spec_i


def hyena_layer(cfg, xa, mod, w_in, b_in, conv_w, conv_b, f_w1, f_b1, f_fr1, f_w2, f_b2, f_fr2, f_w3,
                bias, w_out, b_out, lng, lnb, ctx_out):
    d = cfg.D
    n_ct = d // HY_C_TILE
    z = fused_linear(xa, w_in.astype(BF16), tm=cfg.tm, tn=3 * d // 2, out_defs=[(3 * d, 3 * d // 2, BF16)],
                     epilogue=lambda acc, rows, cols, m: [acc + cols[0]], seg_fn=_seg_fn(cfg), mod=mod,
                     mod_rows=(0, 1), col_extras=[b_in.reshape(1, -1)],
                     n_rows=None if ctx_out else cfg.NL)[0]
    conv_b2 = conv_b.reshape(1, -1)
    seqs = [(cfg.L, cfg.B, 0)]
    if ctx_out:
        seqs.append((cfg.CL, cfg.B, cfg.NL // cfg.CL))
    ys = []
    for length, n_seq, row_blk0 in seqs:
        mats = _fft_mats(*_fft_sizes(length))
        spec_r, spec_i = _hyena_spectra(length, d, f_w1, f_b1, f_fr1, f_w2, f_b2, f_fr2, f_w3, mats)
        y1 = _hyena_conv(None, z, 2 * n_ct, 0, conv_w, conv_b2, bias[0:1], spec_r, spec_i, 0,
                         mats, length, n_seq, row_blk0, F32)
        ys.append(_hyena_conv(y1, z, 0, n_ct, conv_w, conv_b2, bias[1:2], spec_r, spec_i, n_ct,
                              mats, length, n_seq, row_blk0, BF16))
    y = ys[0] if len(ys) == 1 else jnp.concatenate(ys, 0)
    n_rows = None if ctx_out else cfg.NL
    return residual_ln_linear(cfg, xa, y, w_out.astype(BF16), mod, 2, lng, lnb, bias=b_out, n_rows=n_rows)


def ada_modulation(cfg, cvec, ada_w, ada_b, layer):
    d = cfg.D
    out = fused_linear(cvec, ada_w, w_layer=layer, tm=cvec.shape[0], tn=d, out_defs=[(6 * d, d, F32)],
                       epilogue=lambda acc, rows, cols, m: [acc + cols[0]], pre=jax.nn.silu,
                       col_extras=[ada_b[layer].reshape(1, -1)])[0]
    return out[:cfg.B + 1].reshape(cfg.B + 1, 6, d)


def kernel(x, c, ctx, c_ctx, ada_w, ada_b, ln_g, ln_b, mla_w_dq, mla_q_norm, mla_w_uq, mla_w_dkv, mla_kv_norm,
           mla_w_ukv, mla_w_o, hy_w_in, hy_b_in, hy_conv_w, hy_conv_b, hy_f_w1, hy_f_b1, hy_f_fr1, hy_f_w2,
           hy_f_b2, hy_f_fr2, hy_f_w3, hy_bias, hy_w_out, hy_b_out, gqa_w_qkv, gqa_q_norm, gqa_k_norm, gqa_w_o,
           peer_w_q, peer_keys1, peer_keys2, peer_u, peer_v):
    batch, seq, d = x.shape
    cfg = Cfg(batch, seq, ctx.shape[1], d)
    xa = jnp.concatenate([x.reshape(-1, d), ctx.reshape(-1, d)], 0)
    pad_rows = -(batch + 1) % SUBLANES
    cvec = jnp.concatenate([c, c_ctx[None, :], jnp.zeros((pad_rows, d), F32)], 0)
    peer_w = peer_prepare(peer_w_q, peer_keys1, peer_keys2, peer_u, peer_v)
    ia = ib = ic = 0
    for layer in range(DEPTH):
        last = layer == DEPTH - 1
        mod = ada_modulation(cfg, cvec, ada_w, ada_b, layer)
        lng, lnb = ln_g[layer, 0], ln_b[layer, 0]
        kind = layer % N_MIXERS
        if kind == 0:
            xm = mla_layer(cfg, xa, mod, mla_w_dq[ia], mla_q_norm[ia], mla_w_uq[ia], mla_w_dkv[ia],
                           mla_kv_norm[ia], mla_w_ukv[ia], mla_w_o[ia], lng, lnb, not last)
            ia += 1
        elif kind == 1:
            xm = hyena_layer(cfg, xa, mod, hy_w_in[ib], hy_b_in[ib], hy_conv_w[ib], hy_conv_b[ib], hy_f_w1[ib],
                             hy_f_b1[ib], hy_f_fr1[ib], hy_f_w2[ib], hy_f_b2[ib], hy_f_fr2[ib], hy_f_w3[ib],
                             hy_bias[ib], hy_w_out[ib], hy_b_out[ib], lng, lnb, not last)
            ib += 1
        else:
            xm = gqa_layer(cfg, xa, mod, gqa_w_qkv[ic], gqa_q_norm[ic], gqa_k_norm[ic], gqa_w_o[ic],
                           lng, lnb, not last)
            ic += 1
        xa = peer_layer(cfg, xm, mod, layer, *peer_w, ln_g[layer, 1], ln_b[layer, 1], xm.shape[0])
    return xa[:cfg.NL].reshape(batch, seq, d)
```

```python
import functools
import math

import numpy as np
import jax
import jax.numpy as jnp
from jax import lax
from jax.experimental import pallas as pl
from jax.experimental.pallas import tpu as pltpu

F32 = jnp.float32
BF16 = jnp.bfloat16

DEPTH = 4
GRID_W = 64
N_MIXERS = 3
DEEPNORM_ALPHA = (2 * DEPTH) ** 0.25
LN_EPS = 1e-5
RMS_EPS = 1e-6
ROPE_THETA = 10000.0
MLA_HEADS = 8
MLA_Q_RANK = 768
MLA_KV_RANK = 256
MLA_NOPE = 128
MLA_ROPE = 64
MLA_V = 128
HY_ORDER = 2
HY_EMB = 33
HY_BANDS = (HY_EMB - 1) // 2
HY_SHORT = 3
HY_FAST_DECAY = 0.3
HY_SLOW_DECAY = 1.5
HY_TARGET = 1e-2
HY_SHIFT = 0.05
GQA_Q_HEADS = 8
GQA_KV_HEADS = 2
GQA_HEAD_DIM = 128
PEER_HEADS = 8
PEER_N_KEYS = 128
PEER_TOPK = 16
PEER_D_KEY = 256

LANES = 128
SUBLANES = 8
VMEM_LIMIT = 56 * 1024 * 1024


def _params(sem):
    return pltpu.CompilerParams(dimension_semantics=sem, vmem_limit_bytes=VMEM_LIMIT)


class Cfg:
    def __init__(self, batch, seq, ctx_len, d):
        self.B, self.L, self.CL, self.D = batch, seq, ctx_len, d
        self.NL = batch * seq
        self.NC = batch * ctx_len
        self.NT = self.NL + self.NC
        self.tm = math.gcd(512, math.gcd(seq, self.NC))


def _linear_kernel(*refs, has_mod, mod_rows, n_row, n_col, n_out, pre, epilogue):
    it = iter(refs)
    x_ref = next(it)
    mod_ref = next(it) if has_mod else None
    w_ref = next(it)
    row_refs = [next(it) for _ in range(n_row)]
    col_refs = [next(it) for _ in range(n_col)]
    out_refs = [next(it) for _ in range(n_out)]
    x = x_ref[...]
    mod = mod_ref[...] if has_mod else None
    if pre is not None:
        x = pre(x)
    if mod_rows is not None:
        sh, sc = mod_rows
        x = x * (1.0 + mod[sc:sc + 1, :]) + mod[sh:sh + 1, :]
    acc = jnp.dot(x.astype(BF16), w_ref[...].astype(BF16), preferred_element_type=F32)
    outs = epilogue(acc, [r[...] for r in row_refs], [c[...] for c in col_refs], mod)
    for o_ref, o in zip(out_refs, outs):
        o_ref[...] = o.astype(o_ref.dtype)


def fused_linear(x, w, *, tm, tn, out_defs, epilogue, seg_fn=None, mod=None, mod_rows=None,
                 w_layer=None, row_extras=(), col_extras=(), pre=None, n_rows=None):
    rows, k = x.shape
    if n_rows is not None:
        rows = n_rows
    n = w.shape[-1]
    grid = (rows // tm, n // tn)
    in_specs = [pl.BlockSpec((tm, k), lambda i, j: (i, 0))]
    args = [x]
    if mod is not None:
        in_specs.append(pl.BlockSpec((None,) + mod.shape[1:], lambda i, j: (seg_fn(i), 0, 0)))
        args.append(mod)
    if w_layer is None:
        in_specs.append(pl.BlockSpec((k, tn), lambda i, j: (0, j)))
    else:
        in_specs.append(pl.BlockSpec((None, k, tn), lambda i, j: (w_layer, 0, j)))
    args.append(w)
    for r in row_extras:
        in_specs.append(pl.BlockSpec((tm, r.shape[1]), lambda i, j: (i, 0)))
        args.append(r)
    for c in col_extras:
        in_specs.append(pl.BlockSpec((c.shape[0], tn), lambda i, j: (0, j)))
        args.append(c)
    out_specs = [pl.BlockSpec((tm, bc), lambda i, j: (i, j)) for (_, bc, _) in out_defs]
    out_shape = [jax.ShapeDtypeStruct((rows, tc), dt) for (tc, _, dt) in out_defs]
    kern = functools.partial(_linear_kernel, has_mod=mod is not None, mod_rows=mod_rows,
                             n_row=len(row_extras), n_col=len(col_extras), n_out=len(out_defs),
                             pre=pre, epilogue=epilogue)
    outs = pl.pallas_call(kern, grid=grid, in_specs=in_specs, out_specs=out_specs, out_shape=out_shape,
                          compiler_params=_params(("parallel", "parallel")))(*args)
    return outs


def _linear_t_kernel(*refs, has_mod, mod_rows):
    if has_mod:
        x_ref, mod_ref, w_ref, o_ref = refs
        mod = mod_ref[...]
        sh, sc = mod_rows
        x = (x_ref[...] * (1.0 + mod[sc:sc + 1, :]) + mod[sh:sh + 1, :]).astype(BF16)
    else:
        x_ref, w_ref, o_ref = refs
        x = x_ref[...].astype(BF16)
    o_ref[...] = lax.dot_general(w_ref[...], x, (((0,), (1,)), ((), ())),
                                 preferred_element_type=F32).astype(o_ref.dtype)


def linear_transposed(x, w, *, tm, seg_fn=None, mod=None, mod_rows=None):
    rows, k = x.shape
    n = w.shape[1]
    in_specs = [pl.BlockSpec((tm, k), lambda i: (i, 0))]
    args = [x]
    if mod is not None:
        in_specs.append(pl.BlockSpec((None,) + mod.shape[1:], lambda i: (seg_fn(i), 0, 0)))
        args.append(mod)
    in_specs.append(pl.BlockSpec((k, n), lambda i: (0, 0)))
    args.append(w)
    return pl.pallas_call(
        functools.partial(_linear_t_kernel, has_mod=mod is not None, mod_rows=mod_rows),
        grid=(rows // tm,), in_specs=in_specs, out_specs=pl.BlockSpec((n, tm), lambda i: (0, i)),
        out_shape=jax.ShapeDtypeStruct((n, rows), BF16), compiler_params=_params(("parallel",)))(*args)


def _rms(y, gain):
    return y * lax.rsqrt(jnp.mean(jnp.square(y), axis=-1, keepdims=True) + RMS_EPS) * gain


def _rope(y, cos, sin):
    return y * cos + pltpu.roll(y, LANES // 2, 1) * sin


def _layer_norm(v, g, b):
    mu = jnp.mean(v, axis=-1, keepdims=True)
    vc = v - mu
    var = jnp.mean(jnp.square(vc), axis=-1, keepdims=True)
    return vc * lax.rsqrt(var + LN_EPS) * g + b


def residual_ln_linear(cfg, x, y, w, mod, gate_row, lng, lnb, bias=None, n_rows=None):
    d = cfg.D

    def epi(acc, rows, cols, m):
        if bias is not None:
            acc = acc + cols[2]
        v = DEEPNORM_ALPHA * rows[0] + m[gate_row:gate_row + 1, :] * acc
        return [_layer_norm(v, cols[0], cols[1])]

    cols = [lng.reshape(1, d), lnb.reshape(1, d)] + ([bias.reshape(1, d)] if bias is not None else [])
    return fused_linear(y, w, tm=cfg.tm, tn=d, out_defs=[(d, d, F32)], epilogue=epi,
                        seg_fn=_seg_fn(cfg), mod=mod, row_extras=[x], col_extras=cols, n_rows=n_rows)[0]


def _seg_fn(cfg):
    tiles_per_batch = cfg.L // cfg.tm
    nb = cfg.B
    return lambda i: jnp.minimum(i // tiles_per_batch, nb)


ATTN_KEY_CHUNK = 256
ATTN_CHUNK_UNROLL = 8


def _attn_kernel(*refs, n_parts, n_segs, heads, tq):
    q_ref = refs[0]
    pos = 1
    segs = []
    for _ in range(n_segs):
        segs.append((refs[pos:pos + n_parts], refs[pos + n_parts]))
        pos += n_parts + 1
    o_ref, sc_scr, acc_scr = refs[pos], refs[pos + 1], refs[pos + 2]
    width = n_parts * LANES
    qv = q_ref[...]
    q = qv if heads == 1 else jnp.concatenate([qv[:, i * width:(i + 1) * width] for i in range(heads)], 0)
    m = None
    row0 = 0
    for k_refs, _ in segs:
        k = k_refs[0][...] if n_parts == 1 else jnp.concatenate([r[...] for r in k_refs], 1)
        sc = lax.dot_general(k, q, (((1,), (1,)), ((), ())), preferred_element_type=F32)
        sc_scr[pl.ds(row0, sc.shape[0]), :] = sc
        row0 += sc.shape[0]
        ms = jnp.max(sc, axis=0, keepdims=True)
        m = ms if m is None else jnp.maximum(m, ms)
    ones = jnp.ones((2 * SUBLANES, ATTN_KEY_CHUNK), BF16)
    acc_scr[...] = jnp.zeros_like(acc_scr)
    row0 = 0
    for _, vt_ref in segs:
        def chunk(c, carry, vt_ref=vt_ref, row0=row0):
            keys = pl.ds(pl.multiple_of(row0 + c * ATTN_KEY_CHUNK, ATTN_KEY_CHUNK), ATTN_KEY_CHUNK)
            cols = pl.ds(pl.multiple_of(c * ATTN_KEY_CHUNK, ATTN_KEY_CHUNK), ATTN_KEY_CHUNK)
            p = jnp.exp((sc_scr[keys, :] - m).astype(BF16))
            vt = jnp.concatenate([vt_ref[:, cols], ones], 0)
            acc_scr[...] += jnp.dot(vt, p, preferred_element_type=F32)
            return carry

        n_chunks = vt_ref.shape[1] // ATTN_KEY_CHUNK
        lax.fori_loop(0, n_chunks, chunk, 0, unroll=min(n_chunks, ATTN_CHUNK_UNROLL))
        row0 += vt_ref.shape[1]
    acc = acc_scr[...]
    o = (acc[:LANES] / acc[LANES:LANES + 1]).T
    if heads > 1:
        o = jnp.concatenate([o[i * tq:(i + 1) * tq] for i in range(heads)], 1)
    o_ref[...] = o.astype(o_ref.dtype)


def attention(q_arr, q_rows, out_rows, key_segs, *, n_parts, heads, n_batch, n_groups, n_qt, tq, out_rows_total):
    qw = heads * n_parts * LANES
    in_specs = [pl.BlockSpec((tq, qw), lambda b, g, t: (q_rows(b, t), g))]
    args = [q_arr]
    for keys_blk, blk_fn, k_parts, (vt_arr, vt_row) in key_segs:
        for k_arr, k_col in k_parts:
            in_specs.append(pl.BlockSpec((keys_blk, LANES),
                                         lambda b, g, t, blk_fn=blk_fn, k_col=k_col: (blk_fn(b), k_col(g))))
            args.append(k_arr)
        in_specs.append(pl.BlockSpec((LANES, keys_blk),
                                     lambda b, g, t, blk_fn=blk_fn, vt_row=vt_row: (vt_row(g), blk_fn(b))))
        args.append(vt_arr)
    out_spec = pl.BlockSpec((tq, heads * LANES), lambda b, g, t: (out_rows(b, t), g))
    kern = functools.partial(_attn_kernel, n_parts=n_parts, n_segs=len(key_segs), heads=heads, tq=tq)
    total_keys = sum(seg[0] for seg in key_segs)
    return pl.pallas_call(
        kern, grid=(n_batch, n_groups, n_qt), in_specs=in_specs, out_specs=out_spec,
        scratch_shapes=[pltpu.VMEM((total_keys, heads * tq), F32),
                        pltpu.VMEM((LANES + 2 * SUBLANES, heads * tq), F32)],
        out_shape=jax.ShapeDtypeStruct((out_rows_total, n_groups * heads * LANES), BF16),
        compiler_params=_params(("parallel", "parallel", "arbitrary")))(*args)


def _attend_both(cfg, q_arr, k_parts, vt_part, n_parts, heads, n_groups, tq, ctx_out):
    tq = min(tq, cfg.L)
    n_qt = cfg.L // tq
    lat_blocks_ctx = cfg.NL // cfg.CL
    seg_lat = (cfg.L, lambda b: b, k_parts, vt_part)
    seg_ctx = (cfg.CL, lambda b: lat_blocks_ctx + b, k_parts, vt_part)
    lat_rows = lambda b, t: b * n_qt + t
    y_lat = attention(q_arr, lat_rows, lat_rows, [seg_ctx, seg_lat], n_parts=n_parts, heads=heads,
                      n_batch=cfg.B, n_groups=n_groups, n_qt=n_qt, tq=tq, out_rows_total=cfg.NL)
    if not ctx_out:
        return y_lat
    y_ctx = attention(q_arr, lambda b, t: lat_blocks_ctx + b, lambda b, t: b, [seg_ctx], n_parts=n_parts,
                      heads=heads, n_batch=cfg.B, n_groups=n_groups, n_qt=1, tq=cfg.CL, out_rows_total=cfg.NC)
    return jnp.concatenate([y_lat, y_ctx], axis=0)


def _rope_tables(cfg, rot_dim, half_width):
    rows = cfg.L // GRID_W
    row = jnp.repeat(jnp.arange(rows, dtype=F32), GRID_W)
    col = jnp.tile(jnp.arange(GRID_W, dtype=F32), rows)
    quarter = rot_dim // 4
    inv_freq = ROPE_THETA ** (-jnp.arange(quarter, dtype=F32) / quarter)
    ang = jnp.concatenate([row[:, None] * inv_freq, col[:, None] * inv_freq], -1)
    c, s = jnp.cos(ang), jnp.sin(ang)
    pad = half_width - rot_dim // 2
    one = jnp.ones((cfg.L, pad), F32)
    zero = jnp.zeros((cfg.L, pad), F32)
    cos_t = jnp.concatenate([c, one, c, one], -1)
    sin_t = jnp.concatenate([-s, zero, s, zero], -1)
    cos_t = jnp.concatenate([jnp.tile(cos_t, (cfg.B, 1)), jnp.ones((cfg.NC, LANES), F32)], 0)
    sin_t = jnp.concatenate([jnp.tile(sin_t, (cfg.B, 1)), jnp.zeros((cfg.NC, LANES), F32)], 0)
    return cos_t, sin_t


def _deinterleave_cols(w, rot_dim):
    half = rot_dim // 2
    pad = LANES // 2 - half
    ev = w[..., 0::2]
    od = w[..., 1::2]
    z = jnp.zeros(w.shape[:-1] + (pad,), w.dtype)
    return jnp.concatenate([ev, z, od, z], -1)


def mla_layer(cfg, xa, mod, w_dq, q_norm, w_uq, w_dkv, kv_norm, w_ukv, w_o, lng, lnb, ctx_out):
    d = cfg.D
    tm = cfg.tm
    seg = _seg_fn(cfg)
    scale = (MLA_NOPE + MLA_ROPE) ** -0.5
    cos_t, sin_t = _rope_tables(cfg, MLA_ROPE, LANES // 2)
    hq = MLA_HEADS

    w_uq3 = w_uq.reshape(MLA_Q_RANK, hq, MLA_NOPE + MLA_ROPE)
    w_uq_p = jnp.concatenate([w_uq3[..., :MLA_NOPE], _deinterleave_cols(w_uq3[..., MLA_NOPE:], MLA_ROPE)], -1)
    w_uq_p = w_uq_p.reshape(MLA_Q_RANK, hq * 2 * LANES).astype(BF16)
    w_dkv_p = jnp.concatenate([w_dkv[:, :MLA_KV_RANK], _deinterleave_cols(w_dkv[:, MLA_KV_RANK:], MLA_ROPE)],
                              -1).astype(BF16)

    qr, kr = MLA_Q_RANK, MLA_KV_RANK

    def epi_down(acc, rows, cols, m):
        g = cols[0]
        return [_rms(acc[:, :qr], g[:, :qr]),
                _rms(acc[:, qr:qr + kr], g[:, qr:qr + kr]),
                _rope(acc[:, qr + kr:], rows[0], rows[1])]

    w_down = jnp.concatenate([w_dq.astype(BF16), w_dkv_p], -1)
    gains = jnp.concatenate([q_norm, kv_norm, jnp.ones((LANES,), F32)]).reshape(1, -1)
    n_down = qr + kr + LANES
    cq, ckv, kpe = fused_linear(xa, w_down, tm=tm, tn=n_down,
                                out_defs=[(qr, qr, BF16), (kr, kr, BF16), (LANES, LANES, BF16)],
                                epilogue=epi_down, seg_fn=seg, mod=mod, mod_rows=(0, 1),
                                row_extras=[cos_t, sin_t], col_extras=[gains])

    heads_per_tile = 4

    def epi_q(acc, rows, cols, m):
        parts = []
        for hh in range(heads_per_tile):
            c0 = hh * 2 * LANES
            parts.append(acc[:, c0:c0 + LANES] * scale)
            parts.append(_rope(acc[:, c0 + LANES:c0 + 2 * LANES] * scale, rows[0], rows[1]))
        return [jnp.concatenate(parts, -1)]

    tn_q = heads_per_tile * 2 * LANES
    q = fused_linear(cq, w_uq_p, tm=tm, tn=tn_q, out_defs=[(hq * 2 * LANES, tn_q, BF16)],
                     epilogue=epi_q, row_extras=[cos_t, sin_t])[0]

    w_ukv3 = w_ukv.astype(BF16).reshape(kr, hq, MLA_NOPE + MLA_V)
    w_uk = w_ukv3[:, :, :MLA_NOPE].reshape(kr, hq * MLA_NOPE)
    w_uv = w_ukv3[:, :, MLA_NOPE:].reshape(kr, hq * MLA_V)
    kn = fused_linear(ckv, w_uk, tm=tm, tn=hq * MLA_NOPE, out_defs=[(hq * MLA_NOPE, hq * MLA_NOPE, BF16)],
                      epilogue=lambda acc, rows, cols, m: [acc])[0]
    vt = linear_transposed(ckv, w_uv, tm=tm)
    y = _attend_both(cfg, q, [(kn, lambda h: h), (kpe, lambda h: 0)], (vt, lambda h: h),
                     n_parts=2, heads=1, n_groups=hq, tq=512, ctx_out=ctx_out)
    n_rows = None if ctx_out else cfg.NL
    return residual_ln_linear(cfg, xa, y, w_o.astype(BF16), mod, 2, lng, lnb, n_rows=n_rows)


def gqa_layer(cfg, xa, mod, w_qkv, q_norm, k_norm, w_o, lng, lnb, ctx_out):
    tm = cfg.tm
    seg = _seg_fn(cfg)
    hd = GQA_HEAD_DIM
    nq, nk = GQA_Q_HEADS, GQA_KV_HEADS
    scale = hd ** -0.5
    cos_t, sin_t = _rope_tables(cfg, hd, LANES // 2)
    perm = np.concatenate([np.arange(0, hd, 2), np.arange(1, hd, 2)])
    w3 = w_qkv.reshape(cfg.D, nq + 2 * nk, hd)
    w_qk = w3[:, :nq + nk, perm].reshape(cfg.D, -1).astype(BF16)
    w_v = w3[:, nq + nk:, :].reshape(cfg.D, -1).astype(BF16)
    gains = jnp.stack([q_norm[perm], k_norm[perm]])
    n_chunks = nq + nk

    def epi(acc, rows, cols, m):
        g = cols[0]
        outs = []
        for c in range(n_chunks):
            y = acc[:, c * hd:(c + 1) * hd]
            if c < nq:
                y = _rope(_rms(y, g[0:1, :hd]) * scale, rows[0], rows[1])
            else:
                y = _rope(_rms(y, g[1:2, :hd]), rows[0], rows[1])
            outs.append(y)
        return [jnp.concatenate(outs, -1)]

    n_out = n_chunks * hd
    gains_full = jnp.tile(gains, (1, n_chunks))
    qkv = fused_linear(xa, w_qk, tm=tm, tn=n_out, out_defs=[(n_out, n_out, BF16)], epilogue=epi,
                       seg_fn=seg, mod=mod, mod_rows=(0, 1), row_extras=[cos_t, sin_t],
                       col_extras=[gains_full])[0]
    grp = nq // nk
    vt = linear_transposed(xa, w_v, tm=tm, seg_fn=seg, mod=mod, mod_rows=(0, 1))
    y = _attend_both(cfg, qkv, [(qkv, lambda g: nq + g)], (vt, lambda g: g),
                     n_parts=1, heads=grp, n_groups=nk, tq=128, ctx_out=ctx_out)
    n_rows = None if ctx_out else cfg.NL
    return residual_ln_linear(cfg, xa, y, w_o.astype(BF16), mod, 2, lng, lnb, n_rows=n_rows)


PEER_E1_PER_STEP = 16
PEER_E1_PER_DOT = 8
GATE_STEP = 128.0
_PAIRS = [(a, b) for a in range(PEER_TOPK) for b in range(PEER_TOPK) if (a + 1) * (b + 1) <= PEER_TOPK]


def _insert_sorted(lst, x):
    out = []
    for a in lst:
        out.append(jnp.maximum(a, x))
        x = jnp.minimum(a, x)
    return out


def _gelu(x):
    return 0.5 * x * (1.0 + lax.erf(x * (2.0 ** -0.5)))


def _peer_route_group(s1_scr, s2_scr, cnt_scr, r2_scr, g):
    nk = PEER_N_KEYS
    k = PEER_TOPK
    neg = jnp.full((PEER_HEADS, LANES), -jnp.inf, F32)

    def key_rows(i):
        return pl.ds(pl.multiple_of(i * PEER_HEADS, PEER_HEADS), PEER_HEADS)

    def top_values(load):
        return lax.fori_loop(0, nk, lambda i, lst: tuple(_insert_sorted(lst, load(i))), (neg,) * k, unroll=4)

    v1 = top_values(lambda i: s1_scr[g, key_rows(i), :])
    v2 = top_values(lambda i: s2_scr[g, key_rows(i), :])
    cand = {ab: v1[ab[0]] + v2[ab[1]] for ab in _PAIRS}
    top = [neg] * k
    for ab in _PAIRS:
        top = _insert_sorted(top, cand[ab])
    tau = top[k - 1]
    cmax = cand[(0, 0)]
    zsum = jnp.zeros_like(tau)
    for ab in _PAIRS:
        zsum = zsum + jnp.where(cand[ab] >= tau, jnp.exp(cand[ab] - cmax), 0.0)
    inv_z = 1.0 / zsum
    m1, m2 = v1[0], v2[0]

    def finish(i, carry):
        rows = key_rows(i)
        x1 = s1_scr[g, rows, :]
        x2 = s2_scr[g, rows, :]
        cnt = jnp.zeros_like(x1)
        r2 = jnp.zeros_like(x2)
        for b in range(k):
            cnt = cnt + jnp.where(x1 + v2[b] >= tau, 1.0, 0.0)
            r2 = r2 + jnp.where(v2[b] > x2, 1.0, 0.0)
        cnt_scr[g, rows, :] = cnt * GATE_STEP
        r2_scr[g, rows, :] = r2 * GATE_STEP
        s1_scr[g, rows, :] = jnp.exp(x1 - m1) * inv_z
        s2_scr[g, rows, :] = jnp.exp(x2 - m2)
        return carry

    lax.fori_loop(0, nk, finish, 0, unroll=2)


def _peer_kernel(x_ref, mod_ref, wq_ref, k1_ref, k2_ref, *rest, tokens):
    n_sub = PEER_E1_PER_STEP // PEER_E1_PER_DOT
    u_refs, vt_refs = rest[:n_sub], rest[n_sub:2 * n_sub]
    (lng_ref, lnb_ref, o_ref, ht_scr, s1_scr, s2_scr, cnt_scr, r2_scr, r2b_scr, p2b_scr, act_scr,
     acc_scr) = rest[2 * n_sub:]
    e = pl.program_id(1)
    n_groups = tokens // LANES
    nh = PEER_HEADS
    nk = PEER_N_KEYS
    pack = 2 * SUBLANES

    @pl.when(e == 0)
    def _route():
        mod = mod_ref[...]
        h = x_ref[...] * (1.0 + mod[4:5, :]) + mod[3:4, :]
        ht = h.T.astype(BF16)
        ht_scr[...] = ht
        qt = jnp.dot(wq_ref[...], ht, preferred_element_type=F32)
        half_rows = nh * (PEER_D_KEY // 2)
        s1 = jnp.dot(k1_ref[...], qt[:half_rows].astype(BF16), preferred_element_type=F32)
        s2 = jnp.dot(k2_ref[...], qt[half_rows:].astype(BF16), preferred_element_type=F32)
        for g in range(n_groups):
            s1_scr[g] = s1[:, g * LANES:(g + 1) * LANES]
            s2_scr[g] = s2[:, g * LANES:(g + 1) * LANES]
        for g in range(n_groups):
            _peer_route_group(s1_scr, s2_scr, cnt_scr, r2_scr, g)
        for g in range(n_groups):
            for hd in range(nh):
                r2b_scr[g, hd] = r2_scr[g, pl.ds(hd, nk, stride=nh), :].astype(BF16)
                p2b_scr[g, hd] = s2_scr[g, pl.ds(hd, nk, stride=nh), :].astype(BF16)
        acc_scr[...] = jnp.zeros_like(acc_scr)

    def up_projection(k):
        act = jnp.dot(u_refs[k][...], ht_scr[...], preferred_element_type=F32)
        for g in range(n_groups):
            act_scr[k % 2, g] = act[:, g * LANES:(g + 1) * LANES]

    up_projection(0)
    for k in range(n_sub):
        if k + 1 < n_sub:
            up_projection(k + 1)
        row0 = pl.multiple_of((e * PEER_E1_PER_STEP + k * PEER_E1_PER_DOT) * nh, nh)
        w_cols = []
        for g in range(n_groups):
            cnt = cnt_scr[g, pl.ds(row0, PEER_E1_PER_DOT * nh), :]
            p1 = s1_scr[g, pl.ds(row0, PEER_E1_PER_DOT * nh), :]
            gates = [jnp.zeros((nk // pack, pack, LANES), BF16) for _ in range(PEER_E1_PER_DOT)]
            for hd in range(nh):
                r2t = r2b_scr[g, hd].reshape(nk // pack, pack, LANES)
                p2t = p2b_scr[g, hd].reshape(nk // pack, pack, LANES)
                for jj in range(PEER_E1_PER_DOT):
                    r = jj * nh + hd
                    cb = jnp.broadcast_to(cnt[r:r + 1, :], (pack, LANES)).astype(BF16)
                    pb = jnp.broadcast_to(p1[r:r + 1, :], (pack, LANES)).astype(BF16)
                    gates[jj] = gates[jj] + jnp.minimum(jnp.maximum(cb[None] - r2t, 0), pb[None]) * p2t
            w_parts = []
            for jj in range(PEER_E1_PER_DOT):
                a = _gelu(act_scr[k % 2, g, pl.ds(jj * nk, nk), :].astype(BF16))
                w_parts.append(gates[jj].reshape(nk, LANES) * a)
            w_cols.append(jnp.concatenate(w_parts, axis=0))
        w = jnp.concatenate(w_cols, axis=1)
        acc_scr[...] += lax.dot_general(vt_refs[k][...], w, (((0,), (0,)), ((), ())),
                                        preferred_element_type=F32)

    @pl.when(e == pl.num_programs(1) - 1)
    def _finish():
        mod = mod_ref[...]
        f = acc_scr[...].T
        v = DEEPNORM_ALPHA * x_ref[...] + mod[5:6, :] * f
        o_ref[...] = _layer_norm(v, lng_ref[...], lnb_ref[...])


def peer_layer(cfg, xa, mod, layer, wq_t, k1_bd, k2_bd, u_b, vt_b, lng, lnb, n_rows):
    d = cfg.D
    t = cfg.tm
    n_exp = u_b.shape[1]
    e_tile = PEER_E1_PER_STEP * PEER_N_KEYS
    seg = _seg_fn(cfg)
    nq = wq_t.shape[1]
    nkh = k1_bd.shape[1]
    ng = t // LANES
    in_specs = [
        pl.BlockSpec((t, d), lambda i, e: (i, 0)),
        pl.BlockSpec((None, 6, d), lambda i, e: (seg(i), 0, 0)),
        pl.BlockSpec((None, nq, d), lambda i, e: (layer, 0, 0), pipeline_mode=pl.Buffered(1)),
        pl.BlockSpec((None, nkh, nkh), lambda i, e: (layer, 0, 0), pipeline_mode=pl.Buffered(1)),
        pl.BlockSpec((None, nkh, nkh), lambda i, e: (layer, 0, 0), pipeline_mode=pl.Buffered(1)),
    ]
    n_sub = PEER_E1_PER_STEP // PEER_E1_PER_DOT
    sub_rows = PEER_E1_PER_DOT * PEER_N_KEYS
    in_specs += [pl.BlockSpec((None, sub_rows, d), lambda i, e, k=k: (layer, e * n_sub + k, 0))
                 for k in range(n_sub)]
    in_specs += [pl.BlockSpec((None, sub_rows, d), lambda i, e, k=k: (layer, e * n_sub + k, 0))
                 for k in range(n_sub)]
    in_specs += [
        pl.BlockSpec((1, d), lambda i, e: (0, 0)),
        pl.BlockSpec((1, d), lambda i, e: (0, 0)),
    ]
    scratch = [
        pltpu.VMEM((d, t), BF16),
        pltpu.VMEM((ng, nkh, LANES), F32),
        pltpu.VMEM((ng, nkh, LANES), F32),
        pltpu.VMEM((ng, nkh, LANES), F32),
        pltpu.VMEM((ng, nkh, LANES), F32),
        pltpu.VMEM((ng, PEER_HEADS, PEER_N_KEYS, LANES), BF16),
        pltpu.VMEM((ng, PEER_HEADS, PEER_N_KEYS, LANES), BF16),
        pltpu.VMEM((2, ng, PEER_E1_PER_DOT * PEER_N_KEYS, LANES), F32),
        pltpu.VMEM((d, t), F32),
    ]
    return pl.pallas_call(
        functools.partial(_peer_kernel, tokens=t),
        grid=(n_rows // t, n_exp // e_tile), in_specs=in_specs,
        out_specs=pl.BlockSpec((t, d), lambda i, e: (i, 0)),
        out_shape=jax.ShapeDtypeStruct((n_rows, d), F32), scratch_shapes=scratch,
        compiler_params=_params(("parallel", "arbitrary")))(
            xa, mod, wq_t, k1_bd, k2_bd, *([u_b] * n_sub), *([vt_b] * n_sub), lng.reshape(1, d), lnb.reshape(1, d))


def peer_prepare(peer_w_q, peer_keys1, peer_keys2, peer_u, peer_v):
    depth, d, _ = peer_w_q.shape
    half = PEER_D_KEY // 2
    nh = PEER_HEADS
    wq_t = peer_w_q.reshape(depth, d, nh, 2, half).transpose(0, 3, 2, 4, 1).reshape(depth, 2 * nh * half, d)
    eye = jnp.eye(nh, dtype=F32)

    def bd(keys):
        return jnp.einsum('lhkd,hg->lkhgd', keys, eye).reshape(depth, PEER_N_KEYS * nh, nh * half).astype(BF16)

    return (wq_t.astype(BF16), bd(peer_keys1), bd(peer_keys2), peer_u.astype(BF16), peer_v.astype(BF16))


HY_C_TILE = LANES
ROW_PAD = SUBLANES
FFT_UNROLL = 8


def _fft_sizes(length):
    n = 2 * length
    n1 = 64 if n >= 8192 else (32 if n >= 1024 else 16)
    return n1, n // n1


def _fft_mats(n1, n2):
    n = n1 * n2
    two_pi = 2.0 * math.pi
    k2 = jnp.arange(n2, dtype=jnp.int32)
    i1 = jnp.arange(n1, dtype=jnp.int32)
    ang = two_pi * ((k2[:, None] * k2[None, :]) % n2).astype(F32) / n2
    c1 = jnp.concatenate([jnp.cos(ang), -jnp.sin(ang)], 0)
    m = (i1[None, None, :] * k2[:, None, None] + i1[None, None, :] * i1[None, :, None] * n2) % n
    ang = two_pi * m.astype(F32) / n
    gr, gi = jnp.cos(ang), -jnp.sin(ang)
    g = jnp.concatenate([jnp.concatenate([gr, -gi], -1), jnp.concatenate([gi, gr], -1)], -2)
    hr, hi = jnp.swapaxes(gr, 1, 2), -jnp.swapaxes(gi, 1, 2)
    h = jnp.concatenate([jnp.concatenate([hr, -hi], -1), jnp.concatenate([hi, hr], -1)], -2)
    ang = two_pi * ((k2[:n2 // 2, None] * k2[None, :]) % n2).astype(F32) / n2
    c2 = jnp.concatenate([jnp.cos(ang), -jnp.sin(ang)], 1) / n
    return c1.astype(BF16), g.astype(BF16), h.astype(BF16), c2.astype(BF16)


def _fft_stage1(src, c1, a_scr, n1, n2, n2_in):
    pitch = 2 * n2 + ROW_PAD

    def body(i, carry):
        xs = src[pl.ds(i, n2_in, stride=n1), :].astype(BF16)
        a_scr[pl.ds(pl.multiple_of(i * pitch, SUBLANES), 2 * n2), :] = jnp.dot(c1, xs, preferred_element_type=F32)
        return carry

    lax.fori_loop(0, n1, body, 0, unroll=FFT_UNROLL)


def _fft_stage2_load(a_scr, k, n1, n2):
    pitch = 2 * n2 + ROW_PAD
    ar = a_scr[pl.ds(k, n1, stride=pitch), :]
    ai = a_scr[pl.ds(n2 + k, n1, stride=pitch), :]
    return jnp.concatenate([ar, ai], 0).astype(BF16)


def _short_conv(z, w, b):
    rows = z.shape[0]
    idx = lax.broadcasted_iota(jnp.int32, z.shape, 0)
    prev = jnp.where(idx == 0, 0.0, pltpu.roll(z, 1, 0))
    nxt = jnp.where(idx == rows - 1, 0.0, pltpu.roll(z, rows - 1, 0))
    return b + prev * w[0:1, :] + z * w[1:2, :] + nxt * w[2:3, :]


def _hyena_conv_kernel(*refs, first, n1, n2):
    if first:
        (zu_ref, cw_u_ref, cb_u_ref, zx_ref, cw_x_ref, cb_x_ref, bias_ref, sr_ref, si_ref,
         c1_ref, g_ref, h_ref, c2_ref, o_ref, u_scr, y_scr, a_scr, b_scr) = refs
        u_scr[...] = _short_conv(zu_ref[...].astype(F32), cw_u_ref[...], cb_u_ref[...])
    else:
        (zu_ref, zx_ref, cw_x_ref, cb_x_ref, bias_ref, sr_ref, si_ref,
         c1_ref, g_ref, h_ref, c2_ref, o_ref, u_scr, y_scr, a_scr, b_scr) = refs
        u_scr[...] = zu_ref[...]
    _fft_stage1(u_scr, c1_ref[...], a_scr, n1, n2, n2 // 2)
    pitch_b = 2 * n1 + ROW_PAD

    def mid(k, carry):
        x = jnp.dot(g_ref[k], _fft_stage2_load(a_scr, k, n1, n2), preferred_element_type=F32)
        rows = pl.ds(pl.multiple_of(k * n1, n1), n1)
        xr, xi = x[:n1], x[n1:]
        sr, si = sr_ref[rows, :], si_ref[rows, :]
        y = jnp.concatenate([xr * sr - xi * si, xr * si + xi * sr], 0).astype(BF16)
        b_scr[pl.ds(pl.multiple_of(k * pitch_b, SUBLANES), 2 * n1), :] = jnp.dot(
            h_ref[k], y, preferred_element_type=F32)
        return carry

    lax.fori_loop(0, n2, mid, 0, unroll=FFT_UNROLL)

    def last(i, carry):
        br = b_scr[pl.ds(i, n2, stride=pitch_b), :]
        bi = b_scr[pl.ds(n1 + i, n2, stride=pitch_b), :]
        y = jnp.dot(c2_ref[...], jnp.concatenate([br, bi], 0).astype(BF16), preferred_element_type=F32)
        y_scr[pl.ds(i, n2 // 2, stride=n1), :] = y
        return carry

    lax.fori_loop(0, n1, last, 0, unroll=FFT_UNROLL)
    x = _short_conv(zx_ref[...].astype(F32), cw_x_ref[...], cb_x_ref[...])
    u = u_scr[...]
    o_ref[...] = (x * (y_scr[...] + u * bias_ref[...])).astype(o_ref.dtype)


def _hyena_conv(y_prev, z, zu_col0, zx_col0, conv_w, conv_b, bias_row, spec_r, spec_i, spec_col0, mats,
                length, n_seq, row_blk0, out_dtype):
    n1, n2 = _fft_sizes(length)
    c1, g, h, c2 = mats
    c1 = c1[:, :n2 // 2]
    ct = HY_C_TILE
    d = bias_row.shape[1]
    n = 2 * length
    one = pl.Buffered(1)
    first = y_prev is None

    def z_specs(col0):
        return [pl.BlockSpec((length, ct), lambda c, s: (row_blk0 + s, col0 + c)),
                pl.BlockSpec((HY_SHORT, ct), lambda c, s: (0, col0 + c)),
                pl.BlockSpec((1, ct), lambda c, s: (0, col0 + c))]

    if first:
        in_specs = z_specs(zu_col0)
        args = [z, conv_w, conv_b]
    else:
        in_specs = [pl.BlockSpec((length, ct), lambda c, s: (s, c))]
        args = [y_prev]
    in_specs += z_specs(zx_col0) + [
        pl.BlockSpec((1, ct), lambda c, s: (0, c)),
        pl.BlockSpec((n, ct), lambda c, s: (0, spec_col0 + c), pipeline_mode=one),
        pl.BlockSpec((n, ct), lambda c, s: (0, spec_col0 + c), pipeline_mode=one),
        pl.BlockSpec(c1.shape, lambda c, s: (0, 0), pipeline_mode=one),
        pl.BlockSpec(g.shape, lambda c, s: (0, 0, 0), pipeline_mode=one),
        pl.BlockSpec(h.shape, lambda c, s: (0, 0, 0), pipeline_mode=one),
        pl.BlockSpec(c2.shape, lambda c, s: (0, 0), pipeline_mode=one),
    ]
    args += [z, conv_w, conv_b, bias_row, spec_r, spec_i, c1, g, h, c2]
    scratch = [
        pltpu.VMEM((length, ct), F32),
        pltpu.VMEM((length, ct), F32),
        pltpu.VMEM((n1 * (2 * n2 + ROW_PAD), ct), F32),
        pltpu.VMEM((n2 * (2 * n1 + ROW_PAD), ct), F32),
    ]
    return pl.pallas_call(
        functools.partial(_hyena_conv_kernel, first=first, n1=n1, n2=n2),
        grid=(d // ct, n_seq), in_specs=in_specs,
        out_specs=pl.BlockSpec((length, ct), lambda c, s: (s, c)),
        out_shape=jax.ShapeDtypeStruct((n_seq * length, d), out_dtype), scratch_shapes=scratch,
        compiler_params=_params(("parallel", "arbitrary")))(*args)


def _hyena_filter_kernel(emb_ref, w1_ref, b1_ref, fr1_ref, w2_ref, b2_ref, fr2_ref, w3_ref, dl_ref, o_ref):
    hi = lax.Precision.HIGHEST
    emb = emb_ref[...]
    hdn = jnp.sin(fr1_ref[...] * (jnp.dot(emb, w1_ref[...], precision=hi, preferred_element_type=F32) + b1_ref[...]))
    hdn = jnp.sin(fr2_ref[...] * (jnp.dot(hdn, w2_ref[...], precision=hi, preferred_element_type=F32) + b2_ref[...]))
    filt = jnp.dot(hdn, w3_ref[...], precision=hi, preferred_element_type=F32)
    t01 = emb[:, 0:1]
    mask = emb[:, LANES - 1:LANES]
    window = jnp.exp(-t01 * dl_ref[...]) + HY_SHIFT
    o_ref[...] = filt * window * mask


def _hyena_spec_kernel(k_ref, c1_ref, g_ref, sr_ref, si_ref, a_scr, *, n1, n2):
    _fft_stage1(k_ref, c1_ref[...], a_scr, n1, n2, n2)

    def mid(k, carry):
        x = jnp.dot(g_ref[k], _fft_stage2_load(a_scr, k, n1, n2), preferred_element_type=F32)
        rows = pl.ds(pl.multiple_of(k * n1, n1), n1)
        sr_ref[rows, :] = x[:n1]
        si_ref[rows, :] = x[n1:]
        return carry

    lax.fori_loop(0, n2, mid, 0, unroll=FFT_UNROLL)


def _hyena_spectra(length, d, f_w1, f_b1, f_fr1, f_w2, f_b2, f_fr2, f_w3, mats):
    n = 2 * length
    n1, n2 = _fft_sizes(length)
    hid = f_w1.shape[1]
    lag = jnp.concatenate([jnp.arange(length), jnp.zeros((1,), jnp.int32), jnp.arange(length - 1, 0, -1)])
    t01 = jnp.linspace(0.0, 1.0, length, dtype=F32)[:, None]
    w = 2.0 * math.pi * jnp.arange(length, dtype=F32)[:, None] / length
    f = jnp.linspace(1e-4, HY_BANDS - 1, HY_BANDS, dtype=F32)[None, :]
    emb = jnp.concatenate([t01, jnp.cos(f * w), -jnp.sin(f * w)], -1)[lag]
    mask = jnp.ones((n, 1), F32).at[length, 0].set(0.0)
    emb = jnp.concatenate([emb, jnp.zeros((n, LANES - HY_EMB - 1), F32), mask], -1)

    def pad2(a, r, c):
        return jnp.zeros((r, c), F32).at[:a.shape[0], :a.shape[1]].set(a)

    w1 = pad2(f_w1, LANES, LANES)
    b1 = pad2(f_b1[None], 1, LANES)
    fr1 = pad2(f_fr1[None], 1, LANES)
    w2 = pad2(f_w2, LANES, LANES)
    b2 = pad2(f_b2[None], 1, LANES)
    fr2 = pad2(f_fr2[None], 1, LANES)
    w3 = pad2(f_w3, LANES, f_w3.shape[1])
    deltas = jnp.abs(jnp.linspace(math.log(HY_TARGET) / HY_SLOW_DECAY, math.log(HY_TARGET) / HY_FAST_DECAY,
                                  d, dtype=F32))[None, :]
    tr = min(512, length)
    half_tiles = length // tr
    small = lambda shape: pl.BlockSpec(shape, lambda r, o: (0, 0))
    kern = pl.pallas_call(
        _hyena_filter_kernel, grid=(n // tr, HY_ORDER),
        in_specs=[pl.BlockSpec((tr, LANES), lambda r, o: (r, 0)),
                  small((LANES, LANES)), small((1, LANES)), small((1, LANES)),
                  small((LANES, LANES)), small((1, LANES)), small((1, LANES)),
                  pl.BlockSpec((LANES, d), lambda r, o: (0, 2 * o + r // half_tiles)),
                  small((1, d))],
        out_specs=pl.BlockSpec((tr, d), lambda r, o: (r, o)),
        out_shape=jax.ShapeDtypeStruct((n, HY_ORDER * d), F32),
        compiler_params=_params(("parallel", "parallel")))(emb, w1, b1, fr1, w2, b2, fr2, w3, deltas)
    c1, g, _, _ = mats
    ct = HY_C_TILE
    n_cols = HY_ORDER * d
    one = pl.Buffered(1)
    spec_r, spec_i = pl.pallas_call(
        functools.partial(_hyena_spec_kernel, n1=n1, n2=n2), grid=(n_cols // ct,),
        in_specs=[pl.BlockSpec((n, ct), lambda c: (0, c)),
                  pl.BlockSpec(c1.shape, lambda c: (0, 0), pipeline_mode=one),
                  pl.BlockSpec(g.shape, lambda c: (0, 0, 0), pipeline_mode=one)],
        out_specs=[pl.BlockSpec((n, ct), lambda c: (0, c)), pl.BlockSpec((n, ct), lambda c: (0, c))],
        out_shape=[jax.ShapeDtypeStruct((n, n_cols), F32)] * 2,
        scratch_shapes=[pltpu.VMEM((n1 * (2 * n2 + ROW_PAD), ct), F32)],
        compiler_params=_params(("parallel",)))(kern, c1, g)
    return spec_r, spec_i


def hyena_layer(cfg, xa, mod, w_in, b_in, conv_w, conv_b, f_w1, f_b1, f_fr1, f_w2, f_b2, f_fr2, f_w3,
                bias, w_out, b_out, lng, lnb, ctx_out):
    d = cfg.D
    n_ct = d // HY_C_TILE
    z = fused_linear(xa, w_in.astype(BF16), tm=cfg.tm, tn=3 * d // 2, out_defs=[(3 * d, 3 * d // 2, BF16)],
                     epilogue=lambda acc, rows, cols, m: [acc + cols[0]], seg_fn=_seg_fn(cfg), mod=mod,
                     mod_rows=(0, 1), col_extras=[b_in.reshape(1, -1)],
                     n_rows=None if ctx_out else cfg.NL)[0]
    conv_b2 = conv_b.reshape(1, -1)
    seqs = [(cfg.L, cfg.B, 0)]
    if ctx_out:
        seqs.append((cfg.CL, cfg.B, cfg.NL // cfg.CL))
    ys = []
    for length, n_seq, row_blk0 in seqs:
        mats = _fft_mats(*_fft_sizes(length))
        spec_r, spec_i = _hyena_spectra(length, d, f_w1, f_b1, f_fr1, f_w2, f_b2, f_fr2, f_w3, mats)
        y1 = _hyena_conv(None, z, 2 * n_ct, 0, conv_w, conv_b2, bias[0:1], spec_r, spec_i, 0,
                         mats, length, n_seq, row_blk0, F32)
        ys.append(_hyena_conv(y1, z, 0, n_ct, conv_w, conv_b2, bias[1:2], spec_r, spec_i, n_ct,
                              mats, length, n_seq, row_blk0, BF16))
    y = ys[0] if len(ys) == 1 else jnp.concatenate(ys, 0)
    n_rows = None if ctx_out else cfg.NL
    return residual_ln_linear(cfg, xa, y, w_out.astype(BF16), mod, 2, lng, lnb, bias=b_out, n_rows=n_rows)


def ada_modulation(cfg, cvec, ada_w, ada_b, layer):
    d = cfg.D
    out = fused_linear(cvec, ada_w, w_layer=layer, tm=cvec.shape[0], tn=d, out_defs=[(6 * d, d, F32)],
                       epilogue=lambda acc, rows, cols, m: [acc + cols[0]], pre=jax.nn.silu,
                       col_extras=[ada_b[layer].reshape(1, -1)])[0]
    return out[:cfg.B + 1].reshape(cfg.B + 1, 6, d)


def kernel(x, c, ctx, c_ctx, ada_w, ada_b, ln_g, ln_b, mla_w_dq, mla_q_norm, mla_w_uq, mla_w_dkv, mla_kv_norm,
           mla_w_ukv, mla_w_o, hy_w_in, hy_b_in, hy_conv_w, hy_conv_b, hy_f_w1, hy_f_b1, hy_f_fr1, hy_f_w2,
           hy_f_b2, hy_f_fr2, hy_f_w3, hy_bias, hy_w_out, hy_b_out, gqa_w_qkv, gqa_q_norm, gqa_k_norm, gqa_w_o,
           peer_w_q, peer_keys1, peer_keys2, peer_u, peer_v):
    batch, seq, d = x.shape
    cfg = Cfg(batch, seq, ctx.shape[1], d)
    xa = jnp.concatenate([x.reshape(-1, d), ctx.reshape(-1, d)], 0)
    pad_rows = -(batch + 1) % SUBLANES
    cvec = jnp.concatenate([c, c_ctx[None, :], jnp.zeros((pad_rows, d), F32)], 0)
    peer_w = peer_prepare(peer_w_q, peer_keys1, peer_keys2, peer_u, peer_v)
    ia = ib = ic = 0
    for layer in range(DEPTH):
        last = layer == DEPTH - 1
        mod = ada_modulation(cfg, cvec, ada_w, ada_b, layer)
        lng, lnb = ln_g[layer, 0], ln_b[layer, 0]
        kind = layer % N_MIXERS
        if kind == 0:
            xm = mla_layer(cfg, xa, mod, mla_w_dq[ia], mla_q_norm[ia], mla_w_uq[ia], mla_w_dkv[ia],
                           mla_kv_norm[ia], mla_w_ukv[ia], mla_w_o[ia], lng, lnb, not last)
            ia += 1
        elif kind == 1:
            xm = hyena_layer(cfg, xa, mod, hy_w_in[ib], hy_b_in[ib], hy_conv_w[ib], hy_conv_b[ib], hy_f_w1[ib],
                             hy_f_b1[ib], hy_f_fr1[ib], hy_f_w2[ib], hy_f_b2[ib], hy_f_fr2[ib], hy_f_w3[ib],
                             hy_bias[ib], hy_w_out[ib], hy_b_out[ib], lng, lnb, not last)
            ib += 1
        else:
            xm = gqa_layer(cfg, xa, mod, gqa_w_qkv[ic], gqa_q_norm[ic], gqa_k_norm[ic], gqa_w_o[ic],
                           lng, lnb, not last)
            ic += 1
        xa = peer_layer(cfg, xm, mod, layer, *peer_w, ln_g[layer, 1], ln_b[layer, 1], xm.shape[0])
    return xa[:cfg.NL].reshape(batch, seq, d)
```

```python
import functools
import math

import numpy as np
import jax
import jax.numpy as jnp
from jax import lax
from jax.experimental import pallas as pl
from jax.experimental.pallas import tpu as pltpu

F32 = jnp.float32
BF16 = jnp.bfloat16

DEPTH = 4
GRID_W = 64
N_MIXERS = 3
DEEPNORM_ALPHA = (2 * DEPTH) ** 0.25
LN_EPS = 1e-5
RMS_EPS = 1e-6
ROPE_THETA = 10000.0
MLA_HEADS = 8
MLA_Q_RANK = 768
MLA_KV_RANK = 256
MLA_NOPE = 128
MLA_ROPE = 64
MLA_V = 128
HY_ORDER = 2
HY_EMB = 33
HY_BANDS = (HY_EMB - 1) // 2
HY_SHORT = 3
HY_FAST_DECAY = 0.3
HY_SLOW_DECAY = 1.5
HY_TARGET = 1e-2
HY_SHIFT = 0.05
GQA_Q_HEADS = 8
GQA_KV_HEADS = 2
GQA_HEAD_DIM = 128
PEER_HEADS = 8
PEER_N_KEYS = 128
PEER_TOPK = 16
PEER_D_KEY = 256

LANES = 128
SUBLANES = 8
VMEM_LIMIT = 56 * 1024 * 1024


def _params(sem):
    return pltpu.CompilerParams(dimension_semantics=sem, vmem_limit_bytes=VMEM_LIMIT)


class Cfg:
    def __init__(self, batch, seq, ctx_len, d):
        self.B, self.L, self.CL, self.D = batch, seq, ctx_len, d
        self.NL = batch * seq
        self.NC = batch * ctx_len
        self.NT = self.NL + self.NC
        self.tm = math.gcd(512, math.gcd(seq, self.NC))


def _linear_kernel(*refs, has_mod, mod_rows, n_row, n_col, n_out, pre, epilogue):
    it = iter(refs)
    x_ref = next(it)
    mod_ref = next(it) if has_mod else None
    w_ref = next(it)
    row_refs = [next(it) for _ in range(n_row)]
    col_refs = [next(it) for _ in range(n_col)]
    out_refs = [next(it) for _ in range(n_out)]
    x = x_ref[...]
    mod = mod_ref[...] if has_mod else None
    if pre is not None:
        x = pre(x)
    if mod_rows is not None:
        sh, sc = mod_rows
        x = x * (1.0 + mod[sc:sc + 1, :]) + mod[sh:sh + 1, :]
    acc = jnp.dot(x.astype(BF16), w_ref[...].astype(BF16), preferred_element_type=F32)
    outs = epilogue(acc, [r[...] for r in row_refs], [c[...] for c in col_refs], mod)
    for o_ref, o in zip(out_refs, outs):
        o_ref[...] = o.astype(o_ref.dtype)


def fused_linear(x, w, *, tm, tn, out_defs, epilogue, seg_fn=None, mod=None, mod_rows=None,
                 w_layer=None, row_extras=(), col_extras=(), pre=None, n_rows=None):
    rows, k = x.shape
    if n_rows is not None:
        rows = n_rows
    n = w.shape[-1]
    grid = (rows // tm, n // tn)
    in_specs = [pl.BlockSpec((tm, k), lambda i, j: (i, 0))]
    args = [x]
    if mod is not None:
        in_specs.append(pl.BlockSpec((None,) + mod.shape[1:], lambda i, j: (seg_fn(i), 0, 0)))
        args.append(mod)
    if w_layer is None:
        in_specs.append(pl.BlockSpec((k, tn), lambda i, j: (0, j)))
    else:
        in_specs.append(pl.BlockSpec((None, k, tn), lambda i, j: (w_layer, 0, j)))
    args.append(w)
    for r in row_extras:
        in_specs.append(pl.BlockSpec((tm, r.shape[1]), lambda i, j: (i, 0)))
        args.append(r)
    for c in col_extras:
        in_specs.append(pl.BlockSpec((c.shape[0], tn), lambda i, j: (0, j)))
        args.append(c)
    out_specs = [pl.BlockSpec((tm, bc), lambda i, j: (i, j)) for (_, bc, _) in out_defs]
    out_shape = [jax.ShapeDtypeStruct((rows, tc), dt) for (tc, _, dt) in out_defs]
    kern = functools.partial(_linear_kernel, has_mod=mod is not None, mod_rows=mod_rows,
                             n_row=len(row_extras), n_col=len(col_extras), n_out=len(out_defs),
                             pre=pre, epilogue=epilogue)
    outs = pl.pallas_call(kern, grid=grid, in_specs=in_specs, out_specs=out_specs, out_shape=out_shape,
                          compiler_params=_params(("parallel", "parallel")))(*args)
    return outs


def _linear_t_kernel(*refs, has_mod, mod_rows):
    if has_mod:
        x_ref, mod_ref, w_ref, o_ref = refs
        mod = mod_ref[...]
        sh, sc = mod_rows
        x = (x_ref[...] * (1.0 + mod[sc:sc + 1, :]) + mod[sh:sh + 1, :]).astype(BF16)
    else:
        x_ref, w_ref, o_ref = refs
        x = x_ref[...].astype(BF16)
    o_ref[...] = lax.dot_general(w_ref[...], x, (((0,), (1,)), ((), ())),
                                 preferred_element_type=F32).astype(o_ref.dtype)


def linear_transposed(x, w, *, tm, seg_fn=None, mod=None, mod_rows=None):
    rows, k = x.shape
    n = w.shape[1]
    in_specs = [pl.BlockSpec((tm, k), lambda i: (i, 0))]
    args = [x]
    if mod is not None:
        in_specs.append(pl.BlockSpec((None,) + mod.shape[1:], lambda i: (seg_fn(i), 0, 0)))
        args.append(mod)
    in_specs.append(pl.BlockSpec((k, n), lambda i: (0, 0)))
    args.append(w)
    return pl.pallas_call(
        functools.partial(_linear_t_kernel, has_mod=mod is not None, mod_rows=mod_rows),
        grid=(rows // tm,), in_specs=in_specs, out_specs=pl.BlockSpec((n, tm), lambda i: (0, i)),
        out_shape=jax.ShapeDtypeStruct((n, rows), BF16), compiler_params=_params(("parallel",)))(*args)


def _rms(y, gain):
    return y * lax.rsqrt(jnp.mean(jnp.square(y), axis=-1, keepdims=True) + RMS_EPS) * gain


def _rope(y, cos, sin):
    return y * cos + pltpu.roll(y, LANES // 2, 1) * sin


def _layer_norm(v, g, b):
    mu = jnp.mean(v, axis=-1, keepdims=True)
    vc = v - mu
    var = jnp.mean(jnp.square(vc), axis=-1, keepdims=True)
    return vc * lax.rsqrt(var + LN_EPS) * g + b


def residual_ln_linear(cfg, x, y, w, mod, gate_row, lng, lnb, bias=None, n_rows=None):
    d = cfg.D

    def epi(acc, rows, cols, m):
        if bias is not None:
            acc = acc + cols[2]
        v = DEEPNORM_ALPHA * rows[0] + m[gate_row:gate_row + 1, :] * acc
        return [_layer_norm(v, cols[0], cols[1])]

    cols = [lng.reshape(1, d), lnb.reshape(1, d)] + ([bias.reshape(1, d)] if bias is not None else [])
    return fused_linear(y, w, tm=cfg.tm, tn=d, out_defs=[(d, d, F32)], epilogue=epi,
                        seg_fn=_seg_fn(cfg), mod=mod, row_extras=[x], col_extras=cols, n_rows=n_rows)[0]


def _seg_fn(cfg):
    tiles_per_batch = cfg.L // cfg.tm
    nb = cfg.B
    return lambda i: jnp.minimum(i // tiles_per_batch, nb)


ATTN_KEY_CHUNK = 256
ATTN_CHUNK_UNROLL = 8


def _attn_kernel(*refs, n_parts, n_segs, heads, tq):
    q_ref = refs[0]
    pos = 1
    segs = []
    for _ in range(n_segs):
        segs.append((refs[pos:pos + n_parts], refs[pos + n_parts]))
        pos += n_parts + 1
    o_ref, sc_scr, acc_scr = refs[pos], refs[pos + 1], refs[pos + 2]
    width = n_parts * LANES
    qv = q_ref[...]
    q = qv if heads == 1 else jnp.concatenate([qv[:, i * width:(i + 1) * width] for i in range(heads)], 0)
    m = None
    row0 = 0
    for k_refs, _ in segs:
        k = k_refs[0][...] if n_parts == 1 else jnp.concatenate([r[...] for r in k_refs], 1)
        sc = lax.dot_general(k, q, (((1,), (1,)), ((), ())), preferred_element_type=F32)
        sc_scr[pl.ds(row0, sc.shape[0]), :] = sc
        row0 += sc.shape[0]
        ms = jnp.max(sc, axis=0, keepdims=True)
        m = ms if m is None else jnp.maximum(m, ms)
    ones = jnp.ones((2 * SUBLANES, ATTN_KEY_CHUNK), BF16)
    acc_scr[...] = jnp.zeros_like(acc_scr)
    row0 = 0
    for _, vt_ref in segs:
        def chunk(c, carry, vt_ref=vt_ref, row0=row0):
            keys = pl.ds(pl.multiple_of(row0 + c * ATTN_KEY_CHUNK, ATTN_KEY_CHUNK), ATTN_KEY_CHUNK)
            cols = pl.ds(pl.multiple_of(c * ATTN_KEY_CHUNK, ATTN_KEY_CHUNK), ATTN_KEY_CHUNK)
            p = jnp.exp((sc_scr[keys, :] - m).astype(BF16))
            vt = jnp.concatenate([vt_ref[:, cols], ones], 0)
            acc_scr[...] += jnp.dot(vt, p, preferred_element_type=F32)
            return carry

        n_chunks = vt_ref.shape[1] // ATTN_KEY_CHUNK
        lax.fori_loop(0, n_chunks, chunk, 0, unroll=min(n_chunks, ATTN_CHUNK_UNROLL))
        row0 += vt_ref.shape[1]
    acc = acc_scr[...]
    o = (acc[:LANES] / acc[LANES:LANES + 1]).T
    if heads > 1:
        o = jnp.concatenate([o[i * tq:(i + 1) * tq] for i in range(heads)], 1)
    o_ref[...] = o.astype(o_ref.dtype)


def attention(q_arr, q_rows, out_rows, key_segs, *, n_parts, heads, n_batch, n_groups, n_qt, tq, out_rows_total):
    qw = heads * n_parts * LANES
    in_specs = [pl.BlockSpec((tq, qw), lambda b, g, t: (q_rows(b, t), g))]
    args = [q_arr]
    for keys_blk, blk_fn, k_parts, (vt_arr, vt_row) in key_segs:
        for k_arr, k_col in k_parts:
            in_specs.append(pl.BlockSpec((keys_blk, LANES),
                                         lambda b, g, t, blk_fn=blk_fn, k_col=k_col: (blk_fn(b), k_col(g))))
            args.append(k_arr)
        in_specs.append(pl.BlockSpec((LANES, keys_blk),
                                     lambda b, g, t, blk_fn=blk_fn, vt_row=vt_row: (vt_row(g), blk_fn(b))))
        args.append(vt_arr)
    out_spec = pl.BlockSpec((tq, heads * LANES), lambda b, g, t: (out_rows(b, t), g))
    kern = functools.partial(_attn_kernel, n_parts=n_parts, n_segs=len(key_segs), heads=heads, tq=tq)
    total_keys = sum(seg[0] for seg in key_segs)
    return pl.pallas_call(
        kern, grid=(n_batch, n_groups, n_qt), in_specs=in_specs, out_specs=out_spec,
        scratch_shapes=[pltpu.VMEM((total_keys, heads * tq), F32),
                        pltpu.VMEM((LANES + 2 * SUBLANES, heads * tq), F32)],
        out_shape=jax.ShapeDtypeStruct((out_rows_total, n_groups * heads * LANES), BF16),
        compiler_params=_params(("parallel", "parallel", "arbitrary")))(*args)


def _attend_both(cfg, q_arr, k_parts, vt_part, n_parts, heads, n_groups, tq, ctx_out):
    tq = min(tq, cfg.L)
    n_qt = cfg.L // tq
    lat_blocks_ctx = cfg.NL // cfg.CL
    seg_lat = (cfg.L, lambda b: b, k_parts, vt_part)
    seg_ctx = (cfg.CL, lambda b: lat_blocks_ctx + b, k_parts, vt_part)
    lat_rows = lambda b, t: b * n_qt + t
    y_lat = attention(q_arr, lat_rows, lat_rows, [seg_ctx, seg_lat], n_parts=n_parts, heads=heads,
                      n_batch=cfg.B, n_groups=n_groups, n_qt=n_qt, tq=tq, out_rows_total=cfg.NL)
    if not ctx_out:
        return y_lat
    y_ctx = attention(q_arr, lambda b, t: lat_blocks_ctx + b, lambda b, t: b, [seg_ctx], n_parts=n_parts,
                      heads=heads, n_batch=cfg.B, n_groups=n_groups, n_qt=1, tq=cfg.CL, out_rows_total=cfg.NC)
    return jnp.concatenate([y_lat, y_ctx], axis=0)


def _rope_tables(cfg, rot_dim, half_width):
    rows = cfg.L // GRID_W
    row = jnp.repeat(jnp.arange(rows, dtype=F32), GRID_W)
    col = jnp.tile(jnp.arange(GRID_W, dtype=F32), rows)
    quarter = rot_dim // 4
    inv_freq = ROPE_THETA ** (-jnp.arange(quarter, dtype=F32) / quarter)
    ang = jnp.concatenate([row[:, None] * inv_freq, col[:, None] * inv_freq], -1)
    c, s = jnp.cos(ang), jnp.sin(ang)
    pad = half_width - rot_dim // 2
    one = jnp.ones((cfg.L, pad), F32)
    zero = jnp.zeros((cfg.L, pad), F32)
    cos_t = jnp.concatenate([c, one, c, one], -1)
    sin_t = jnp.concatenate([-s, zero, s, zero], -1)
    cos_t = jnp.concatenate([jnp.tile(cos_t, (cfg.B, 1)), jnp.ones((cfg.NC, LANES), F32)], 0)
    sin_t = jnp.concatenate([jnp.tile(sin_t, (cfg.B, 1)), jnp.zeros((cfg.NC, LANES), F32)], 0)
    return cos_t, sin_t


def _deinterleave_cols(w, rot_dim):
    half = rot_dim // 2
    pad = LANES // 2 - half
    ev = w[..., 0::2]
    od = w[..., 1::2]
    z = jnp.zeros(w.shape[:-1] + (pad,), w.dtype)
    return jnp.concatenate([ev, z, od, z], -1)


def mla_layer(cfg, xa, mod, w_dq, q_norm, w_uq, w_dkv, kv_norm, w_ukv, w_o, lng, lnb, ctx_out):
    tm = cfg.tm
    seg = _seg_fn(cfg)
    scale = (MLA_NOPE + MLA_ROPE) ** -0.5
    cos_t, sin_t = _rope_tables(cfg, MLA_ROPE, LANES // 2)
    hq = MLA_HEADS

    w_uq3 = w_uq.reshape(MLA_Q_RANK, hq, MLA_NOPE + MLA_ROPE)
    w_uq_p = jnp.concatenate([w_uq3[..., :MLA_NOPE], _deinterleave_cols(w_uq3[..., MLA_NOPE:], MLA_ROPE)], -1)
    w_uq_p = w_uq_p.reshape(MLA_Q_RANK, hq * 2 * LANES).astype(BF16)
    w_dkv_p = jnp.concatenate([w_dkv[:, :MLA_KV_RANK], _deinterleave_cols(w_dkv[:, MLA_KV_RANK:], MLA_ROPE)],
                              -1).astype(BF16)

    qr, kr = MLA_Q_RANK, MLA_KV_RANK

    def epi_down(acc, rows, cols, m):
        g = cols[0]
        return [_rms(acc[:, :qr], g[:, :qr]),
                _rms(acc[:, qr:qr + kr], g[:, qr:qr + kr]),
                _rope(acc[:, qr + kr:], rows[0], rows[1])]

    w_down = jnp.concatenate([w_dq.astype(BF16), w_dkv_p], -1)
    gains = jnp.concatenate([q_norm, kv_norm, jnp.ones((LANES,), F32)]).reshape(1, -1)
    n_down = qr + kr + LANES
    cq, ckv, kpe = fused_linear(xa, w_down, tm=tm, tn=n_down,
                                out_defs=[(qr, qr, BF16), (kr, kr, BF16), (LANES, LANES, BF16)],
                                epilogue=epi_down, seg_fn=seg, mod=mod, mod_rows=(0, 1),
                                row_extras=[cos_t, sin_t], col_extras=[gains])

    heads_per_tile = 4

    def epi_q(acc, rows, cols, m):
        parts = []
        for hh in range(heads_per_tile):
            c0 = hh * 2 * LANES
            parts.append(acc[:, c0:c0 + LANES] * scale)
            parts.append(_rope(acc[:, c0 + LANES:c0 + 2 * LANES] * scale, rows[0], rows[1]))
        return [jnp.concatenate(parts, -1)]

    tn_q = heads_per_tile * 2 * LANES
    q = fused_linear(cq, w_uq_p, tm=tm, tn=tn_q, out_defs=[(hq * 2 * LANES, tn_q, BF16)],
                     epilogue=epi_q, row_extras=[cos_t, sin_t])[0]

    w_ukv3 = w_ukv.astype(BF16).reshape(kr, hq, MLA_NOPE + MLA_V)
    w_uk = w_ukv3[:, :, :MLA_NOPE].reshape(kr, hq * MLA_NOPE)
    w_uv = w_ukv3[:, :, MLA_NOPE:].reshape(kr, hq * MLA_V)
    kn = fused_linear(ckv, w_uk, tm=tm, tn=hq * MLA_NOPE, out_defs=[(hq * MLA_NOPE, hq * MLA_NOPE, BF16)],
                      epilogue=lambda acc, rows, cols, m: [acc])[0]
    vt = linear_transposed(ckv, w_uv, tm=tm)
    y = _attend_both(cfg, q, [(kn, lambda h: h), (kpe, lambda h: 0)], (vt, lambda h: h),
                     n_parts=2, heads=1, n_groups=hq, tq=512, ctx_out=ctx_out)
    n_rows = None if ctx_out else cfg.NL
    return residual_ln_linear(cfg, xa, y, w_o.astype(BF16), mod, 2, lng, lnb, n_rows=n_rows)


def gqa_layer(cfg, xa, mod, w_qkv, q_norm, k_norm, w_o, lng, lnb, ctx_out):
    tm = cfg.tm
    seg = _seg_fn(cfg)
    hd = GQA_HEAD_DIM
    nq, nk = GQA_Q_HEADS, GQA_KV_HEADS
    scale = hd ** -0.5
    cos_t, sin_t = _rope_tables(cfg, hd, LANES // 2)
    perm = np.concatenate([np.arange(0, hd, 2), np.arange(1, hd, 2)])
    w3 = w_qkv.reshape(cfg.D, nq + 2 * nk, hd)
    w_qk = w3[:, :nq + nk, perm].reshape(cfg.D, -1).astype(BF16)
    w_v = w3[:, nq + nk:, :].reshape(cfg.D, -1).astype(BF16)
    gains = jnp.stack([q_norm[perm], k_norm[perm]])
    n_chunks = nq + nk

    def epi(acc, rows, cols, m):
        g = cols[0]
        outs = []
        for c in range(n_chunks):
            y = acc[:, c * hd:(c + 1) * hd]
            if c < nq:
                y = _rope(_rms(y, g[0:1, :hd]) * scale, rows[0], rows[1])
            else:
                y = _rope(_rms(y, g[1:2, :hd]), rows[0], rows[1])
            outs.append(y)
        return [jnp.concatenate(outs, -1)]

    n_out = n_chunks * hd
    gains_full = jnp.tile(gains, (1, n_chunks))
    qkv = fused_linear(xa, w_qk, tm=tm, tn=n_out, out_defs=[(n_out, n_out, BF16)], epilogue=epi,
                       seg_fn=seg, mod=mod, mod_rows=(0, 1), row_extras=[cos_t, sin_t],
                       col_extras=[gains_full])[0]
    grp = nq // nk
    vt = linear_transposed(xa, w_v, tm=tm, seg_fn=seg, mod=mod, mod_rows=(0, 1))
    y = _attend_both(cfg, qkv, [(qkv, lambda g: nq + g)], (vt, lambda g: g),
                     n_parts=1, heads=grp, n_groups=nk, tq=128, ctx_out=ctx_out)
    n_rows = None if ctx_out else cfg.NL
    return residual_ln_linear(cfg, xa, y, w_o.astype(BF16), mod, 2, lng, lnb, n_rows=n_rows)


PEER_E1_PER_STEP = 16
PEER_E1_PER_DOT = 8
GATE_STEP = 128.0
_PAIRS = [(a, b) for a in range(PEER_TOPK) for b in range(PEER_TOPK) if (a + 1) * (b + 1) <= PEER_TOPK]


def _insert_sorted(lst, x):
    out = []
    for a in lst:
        out.append(jnp.maximum(a, x))
        x = jnp.minimum(a, x)
    return out


def _gelu(x):
    return 0.5 * x * (1.0 + lax.erf(x * (2.0 ** -0.5)))


def _peer_route_group(s1_scr, s2_scr, cnt_scr, r2_scr, g):
    nk = PEER_N_KEYS
    k = PEER_TOPK
    neg = jnp.full((PEER_HEADS, LANES), -jnp.inf, F32)

    def key_rows(i):
        return pl.ds(pl.multiple_of(i * PEER_HEADS, PEER_HEADS), PEER_HEADS)

    def top_values(load):
        return lax.fori_loop(0, nk, lambda i, lst: tuple(_insert_sorted(lst, load(i))), (neg,) * k, unroll=4)

    v1 = top_values(lambda i: s1_scr[g, key_rows(i), :])
    v2 = top_values(lambda i: s2_scr[g, key_rows(i), :])
    cand = {ab: v1[ab[0]] + v2[ab[1]] for ab in _PAIRS}
    top = [neg] * k
    for ab in _PAIRS:
        top = _insert_sorted(top, cand[ab])
    tau = top[k - 1]
    cmax = cand[(0, 0)]
    zsum = jnp.zeros_like(tau)
    for ab in _PAIRS:
        zsum = zsum + jnp.where(cand[ab] >= tau, jnp.exp(cand[ab] - cmax), 0.0)
    inv_z = 1.0 / zsum
    m1, m2 = v1[0], v2[0]

    def finish(i, carry):
        rows = key_rows(i)
        x1 = s1_scr[g, rows, :]
        x2 = s2_scr[g, rows, :]
        cnt = jnp.zeros_like(x1)
        r2 = jnp.zeros_like(x2)
        for b in range(k):
            cnt = cnt + jnp.where(x1 + v2[b] >= tau, 1.0, 0.0)
            r2 = r2 + jnp.where(v2[b] > x2, 1.0, 0.0)
        cnt_scr[g, rows, :] = cnt * GATE_STEP
        r2_scr[g, rows, :] = r2 * GATE_STEP
        s1_scr[g, rows, :] = jnp.exp(x1 - m1) * inv_z
        s2_scr[g, rows, :] = jnp.exp(x2 - m2)
        return carry

    lax.fori_loop(0, nk, finish, 0, unroll=2)


def _peer_kernel(x_ref, mod_ref, wq_ref, k1_ref, k2_ref, *rest, tokens):
    n_sub = PEER_E1_PER_STEP // PEER_E1_PER_DOT
    u_refs, vt_refs = rest[:n_sub], rest[n_sub:2 * n_sub]
    (lng_ref, lnb_ref, o_ref, ht_scr, s1_scr, s2_scr, cnt_scr, r2_scr, r2b_scr, p2b_scr, act_scr,
     acc_scr) = rest[2 * n_sub:]
    e = pl.program_id(1)
    n_groups = tokens // LANES
    nh = PEER_HEADS
    nk = PEER_N_KEYS
    pack = 2 * SUBLANES

    @pl.when(e == 0)
    def _route():
        mod = mod_ref[...]
        h = x_ref[...] * (1.0 + mod[4:5, :]) + mod[3:4, :]
        ht = h.T.astype(BF16)
        ht_scr[...] = ht
        qt = jnp.dot(wq_ref[...], ht, preferred_element_type=F32)
        half_rows = nh * (PEER_D_KEY // 2)
        s1 = jnp.dot(k1_ref[...], qt[:half_rows].astype(BF16), preferred_element_type=F32)
        s2 = jnp.dot(k2_ref[...], qt[half_rows:].astype(BF16), preferred_element_type=F32)
        for g in range(n_groups):
            s1_scr[g] = s1[:, g * LANES:(g + 1) * LANES]
            s2_scr[g] = s2[:, g * LANES:(g + 1) * LANES]
        for g in range(n_groups):
            _peer_route_group(s1_scr, s2_scr, cnt_scr, r2_scr, g)
        for g in range(n_groups):
            for hd in range(nh):
                r2b_scr[g, hd] = r2_scr[g, pl.ds(hd, nk, stride=nh), :].astype(BF16)
                p2b_scr[g, hd] = s2_scr[g, pl.ds(hd, nk, stride=nh), :].astype(BF16)
        acc_scr[...] = jnp.zeros_like(acc_scr)

    def up_projection(k):
        act = jnp.dot(u_refs[k][...], ht_scr[...], preferred_element_type=F32)
        for g in range(n_groups):
            act_scr[k % 2, g] = act[:, g * LANES:(g + 1) * LANES]

    up_projection(0)
    for k in range(n_sub):
        if k + 1 < n_sub:
            up_projection(k + 1)
        row0 = pl.multiple_of((e * PEER_E1_PER_STEP + k * PEER_E1_PER_DOT) * nh, nh)
        w_cols = []
        for g in range(n_groups):
            cnt = cnt_scr[g, pl.ds(row0, PEER_E1_PER_DOT * nh), :]
            p1 = s1_scr[g, pl.ds(row0, PEER_E1_PER_DOT * nh), :]
            gates = [jnp.zeros((nk // pack, pack, LANES), BF16) for _ in range(PEER_E1_PER_DOT)]
            for hd in range(nh):
                r2t = r2b_scr[g, hd].reshape(nk // pack, pack, LANES)
                p2t = p2b_scr[g, hd].reshape(nk // pack, pack, LANES)
                for jj in range(PEER_E1_PER_DOT):
                    r = jj * nh + hd
                    cb = jnp.broadcast_to(cnt[r:r + 1, :], (pack, LANES)).astype(BF16)
                    pb = jnp.broadcast_to(p1[r:r + 1, :], (pack, LANES)).astype(BF16)
                    gates[jj] = gates[jj] + jnp.minimum(jnp.maximum(cb[None] - r2t, 0), pb[None]) * p2t
            w_parts = []
            for jj in range(PEER_E1_PER_DOT):
                a = _gelu(act_scr[k % 2, g, pl.ds(jj * nk, nk), :].astype(BF16))
                w_parts.append(gates[jj].reshape(nk, LANES) * a)
            w_cols.append(jnp.concatenate(w_parts, axis=0))
        w = jnp.concatenate(w_cols, axis=1)
        acc_scr[...] += lax.dot_general(vt_refs[k][...], w, (((0,), (0,)), ((), ())),
                                        preferred_element_type=F32)

    @pl.when(e == pl.num_programs(1) - 1)
    def _finish():
        mod = mod_ref[...]
        f = acc_scr[...].T
        v = DEEPNORM_ALPHA * x_ref[...] + mod[5:6, :] * f
        o_ref[...] = _layer_norm(v, lng_ref[...], lnb_ref[...])


def peer_layer(cfg, xa, mod, layer, wq_t, k1_bd, k2_bd, u_b, vt_b, lng, lnb, n_rows):
    d = cfg.D
    t = cfg.tm
    n_exp = u_b.shape[1]
    e_tile = PEER_E1_PER_STEP * PEER_N_KEYS
    seg = _seg_fn(cfg)
    nq = wq_t.shape[1]
    nkh = k1_bd.shape[1]
    ng = t // LANES
    in_specs = [
        pl.BlockSpec((t, d), lambda i, e: (i, 0)),
        pl.BlockSpec((None, 6, d), lambda i, e: (seg(i), 0, 0)),
        pl.BlockSpec((None, nq, d), lambda i, e: (layer, 0, 0), pipeline_mode=pl.Buffered(1)),
        pl.BlockSpec((None, nkh, nkh), lambda i, e: (layer, 0, 0), pipeline_mode=pl.Buffered(1)),
        pl.BlockSpec((None, nkh, nkh), lambda i, e: (layer, 0, 0), pipeline_mode=pl.Buffered(1)),
    ]
    n_sub = PEER_E1_PER_STEP // PEER_E1_PER_DOT
    sub_rows = PEER_E1_PER_DOT * PEER_N_KEYS
    in_specs += [pl.BlockSpec((None, sub_rows, d), lambda i, e, k=k: (layer, e * n_sub + k, 0))
                 for k in range(n_sub)]
    in_specs += [pl.BlockSpec((None, sub_rows, d), lambda i, e, k=k: (layer, e * n_sub + k, 0))
                 for k in range(n_sub)]
    in_specs += [
        pl.BlockSpec((1, d), lambda i, e: (0, 0)),
        pl.BlockSpec((1, d), lambda i, e: (0, 0)),
    ]
    scratch = [
        pltpu.VMEM((d, t), BF16),
        pltpu.VMEM((ng, nkh, LANES), F32),
        pltpu.VMEM((ng, nkh, LANES), F32),
        pltpu.VMEM((ng, nkh, LANES), F32),
        pltpu.VMEM((ng, nkh, LANES), F32),
        pltpu.VMEM((ng, PEER_HEADS, PEER_N_KEYS, LANES), BF16),
        pltpu.VMEM((ng, PEER_HEADS, PEER_N_KEYS, LANES), BF16),
        pltpu.VMEM((2, ng, PEER_E1_PER_DOT * PEER_N_KEYS, LANES), F32),
        pltpu.VMEM((d, t), F32),
    ]
    return pl.pallas_call(
        functools.partial(_peer_kernel, tokens=t),
        grid=(n_rows // t, n_exp // e_tile), in_specs=in_specs,
        out_specs=pl.BlockSpec((t, d), lambda i, e: (i, 0)),
        out_shape=jax.ShapeDtypeStruct((n_rows, d), F32), scratch_shapes=scratch,
        compiler_params=_params(("parallel", "arbitrary")))(
            xa, mod, wq_t, k1_bd, k2_bd, *([u_b] * n_sub), *([vt_b] * n_sub), lng.reshape(1, d), lnb.reshape(1, d))


def peer_prepare(peer_w_q, peer_keys1, peer_keys2, peer_u, peer_v):
    depth, d, _ = peer_w_q.shape
    half = PEER_D_KEY // 2
    nh = PEER_HEADS
    wq_t = peer_w_q.reshape(depth, d, nh, 2, half).transpose(0, 3, 2, 4, 1).reshape(depth, 2 * nh * half, d)
    eye = jnp.eye(nh, dtype=F32)

    def bd(keys):
        return jnp.einsum('lhkd,hg->lkhgd', keys, eye).reshape(depth, PEER_N_KEYS * nh, nh * half).astype(BF16)

    return (wq_t.astype(BF16), bd(peer_keys1), bd(peer_keys2), peer_u.astype(BF16), peer_v.astype(BF16))


HY_C_TILE = LANES
ROW_PAD = SUBLANES
FFT_UNROLL = 16


def _fft_sizes(length):
    n = 2 * length
    n1 = 64 if n >= 8192 else (32 if n >= 1024 else 16)
    return n1, n // n1


def _fft_mats(n1, n2):
    n = n1 * n2
    two_pi = 2.0 * math.pi
    k2 = jnp.arange(n2, dtype=jnp.int32)
    i1 = jnp.arange(n1, dtype=jnp.int32)
    ang = two_pi * ((k2[:, None] * k2[None, :]) % n2).astype(F32) / n2
    c1 = jnp.concatenate([jnp.cos(ang), -jnp.sin(ang)], 0)
    m = (i1[None, None, :] * k2[:, None, None] + i1[None, None, :] * i1[None, :, None] * n2) % n
    ang = two_pi * m.astype(F32) / n
    gr, gi = jnp.cos(ang), -jnp.sin(ang)
    g = jnp.concatenate([jnp.concatenate([gr, -gi], -1), jnp.concatenate([gi, gr], -1)], -2)
    hr, hi = jnp.swapaxes(gr, 1, 2), -jnp.swapaxes(gi, 1, 2)
    h = jnp.concatenate([jnp.concatenate([hr, -hi], -1), jnp.concatenate([hi, hr], -1)], -2)
    ang = two_pi * ((k2[:n2 // 2, None] * k2[None, :]) % n2).astype(F32) / n2
    c2 = jnp.concatenate([jnp.cos(ang), -jnp.sin(ang)], 1) / n
    return c1.astype(BF16), g.astype(BF16), h.astype(BF16), c2.astype(BF16)


def _fft_stage1(src, c1, a_scr, n1, n2, n2_in):
    pitch = 2 * n2 + ROW_PAD

    def body(i, carry):
        xs = src[pl.ds(i, n2_in, stride=n1), :].astype(BF16)
        a_scr[pl.ds(pl.multiple_of(i * pitch, SUBLANES), 2 * n2), :] = jnp.dot(c1, xs, preferred_element_type=F32)
        return carry

    lax.fori_loop(0, n1, body, 0, unroll=FFT_UNROLL)


def _fft_stage2_load(a_scr, k, n1, n2):
    pitch = 2 * n2 + ROW_PAD
    ar = a_scr[pl.ds(k, n1, stride=pitch), :]
    ai = a_scr[pl.ds(n2 + k, n1, stride=pitch), :]
    return jnp.concatenate([ar, ai], 0).astype(BF16)


def _short_conv(z, w, b):
    rows = z.shape[0]
    idx = lax.broadcasted_iota(jnp.int32, z.shape, 0)
    prev = jnp.where(idx == 0, 0.0, pltpu.roll(z, 1, 0))
    nxt = jnp.where(idx == rows - 1, 0.0, pltpu.roll(z, rows - 1, 0))
    return b + prev * w[0:1, :] + z * w[1:2, :] + nxt * w[2:3, :]


def _hyena_conv_kernel(*refs, first, n1, n2):
    if first:
        (zu_ref, cw_u_ref, cb_u_ref, zx_ref, cw_x_ref, cb_x_ref, bias_ref, sr_ref, si_ref,
         c1_ref, g_ref, h_ref, c2_ref, o_ref, u_scr, y_scr, a_scr, b_scr) = refs
        u_scr[...] = _short_conv(zu_ref[...].astype(F32), cw_u_ref[...], cb_u_ref[...])
    else:
        (zu_ref, zx_ref, cw_x_ref, cb_x_ref, bias_ref, sr_ref, si_ref,
         c1_ref, g_ref, h_ref, c2_ref, o_ref, u_scr, y_scr, a_scr, b_scr) = refs
        u_scr[...] = zu_ref[...]
    _fft_stage1(u_scr, c1_ref[...], a_scr, n1, n2, n2 // 2)
    pitch_b = 2 * n1 + ROW_PAD

    def mid(k, carry):
        x = jnp.dot(g_ref[k], _fft_stage2_load(a_scr, k, n1, n2), preferred_element_type=F32)
        rows = pl.ds(pl.multiple_of(k * n1, n1), n1)
        xr, xi = x[:n1], x[n1:]
        sr, si = sr_ref[rows, :], si_ref[rows, :]
        y = jnp.concatenate([xr * sr - xi * si, xr * si + xi * sr], 0).astype(BF16)
        b_scr[pl.ds(pl.multiple_of(k * pitch_b, SUBLANES), 2 * n1), :] = jnp.dot(
            h_ref[k], y, preferred_element_type=F32)
        return carry

    lax.fori_loop(0, n2, mid, 0, unroll=FFT_UNROLL)

    def last(i, carry):
        br = b_scr[pl.ds(i, n2, stride=pitch_b), :]
        bi = b_scr[pl.ds(n1 + i, n2, stride=pitch_b), :]
        y = jnp.dot(c2_ref[...], jnp.concatenate([br, bi], 0).astype(BF16), preferred_element_type=F32)
        y_scr[pl.ds(i, n2 // 2, stride=n1), :] = y
        return carry

    lax.fori_loop(0, n1, last, 0, unroll=FFT_UNROLL)
    x = _short_conv(zx_ref[...].astype(F32), cw_x_ref[...], cb_x_ref[...])
    u = u_scr[...]
    o_ref[...] = (x * (y_scr[...] + u * bias_ref[...])).astype(o_ref.dtype)


def _hyena_conv(y_prev, z, zu_col0, zx_col0, conv_w, conv_b, bias_row, spec_r, spec_i, spec_col0, mats,
                length, n_seq, row_blk0, out_dtype):
    n1, n2 = _fft_sizes(length)
    c1, g, h, c2 = mats
    c1 = c1[:, :n2 // 2]
    ct = HY_C_TILE
    d = bias_row.shape[1]
    n = 2 * length
    one = pl.Buffered(1)
    first = y_prev is None

    def z_specs(col0):
        return [pl.BlockSpec((length, ct), lambda c, s: (row_blk0 + s, col0 + c)),
                pl.BlockSpec((HY_SHORT, ct), lambda c, s: (0, col0 + c)),
                pl.BlockSpec((1, ct), lambda c, s: (0, col0 + c))]

    if first:
        in_specs = z_specs(zu_col0)
        args = [z, conv_w, conv_b]
    else:
        in_specs = [pl.BlockSpec((length, ct), lambda c, s: (s, c))]
        args = [y_prev]
    in_specs += z_specs(zx_col0) + [
        pl.BlockSpec((1, ct), lambda c, s: (0, c)),
        pl.BlockSpec((n, ct), lambda c, s: (0, spec_col0 + c), pipeline_mode=one),
        pl.BlockSpec((n, ct), lambda c, s: (0, spec_col0 + c), pipeline_mode=one),
        pl.BlockSpec(c1.shape, lambda c, s: (0, 0), pipeline_mode=one),
        pl.BlockSpec(g.shape, lambda c, s: (0, 0, 0), pipeline_mode=one),
        pl.BlockSpec(h.shape, lambda c, s: (0, 0, 0), pipeline_mode=one),
        pl.BlockSpec(c2.shape, lambda c, s: (0, 0), pipeline_mode=one),
    ]
    args += [z, conv_w, conv_b, bias_row, spec_r, spec_i, c1, g, h, c2]
    scratch = [
        pltpu.VMEM((length, ct), F32),
        pltpu.VMEM((length, ct), F32),
        pltpu.VMEM((n1 * (2 * n2 + ROW_PAD), ct), F32),
        pltpu.VMEM((n2 * (2 * n1 + ROW_PAD), ct), F32),
    ]
    return pl.pallas_call(
        functools.partial(_hyena_conv_kernel, first=first, n1=n1, n2=n2),
        grid=(d // ct, n_seq), in_specs=in_specs,
        out_specs=pl.BlockSpec((length, ct), lambda c, s: (s, c)),
        out_shape=jax.ShapeDtypeStruct((n_seq * length, d), out_dtype), scratch_shapes=scratch,
        compiler_params=_params(("parallel", "arbitrary")))(*args)


def _hyena_filter_kernel(emb_ref, w1_ref, b1_ref, fr1_ref, w2_ref, b2_ref, fr2_ref, w3_ref, dl_ref, o_ref):
    hi = lax.Precision.HIGHEST
    emb = emb_ref[...]
    hdn = jnp.sin(fr1_ref[...] * (jnp.dot(emb, w1_ref[...], precision=hi, preferred_element_type=F32) + b1_ref[...]))
    hdn = jnp.sin(fr2_ref[...] * (jnp.dot(hdn, w2_ref[...], precision=hi, preferred_element_type=F32) + b2_ref[...]))
    filt = jnp.dot(hdn, w3_ref[...], precision=hi, preferred_element_type=F32)
    t01 = emb[:, 0:1]
    mask = emb[:, LANES - 1:LANES]
    window = jnp.exp(-t01 * dl_ref[...]) + HY_SHIFT
    o_ref[...] = filt * window * mask


def _hyena_spec_kernel(k_ref, c1_ref, g_ref, sr_ref, si_ref, a_scr, *, n1, n2):
    _fft_stage1(k_ref, c1_ref[...], a_scr, n1, n2, n2)

    def mid(k, carry):
        x = jnp.dot(g_ref[k], _fft_stage2_load(a_scr, k, n1, n2), preferred_element_type=F32)
        rows = pl.ds(pl.multiple_of(k * n1, n1), n1)
        sr_ref[rows, :] = x[:n1]
        si_ref[rows, :] = x[n1:]
        return carry

    lax.fori_loop(0, n2, mid, 0, unroll=FFT_UNROLL)


def _hyena_spectra(length, d, f_w1, f_b1, f_fr1, f_w2, f_b2, f_fr2, f_w3, mats):
    n = 2 * length
    n1, n2 = _fft_sizes(length)
    lag = jnp.concatenate([jnp.arange(length), jnp.zeros((1,), jnp.int32), jnp.arange(length - 1, 0, -1)])
    t01 = jnp.linspace(0.0, 1.0, length, dtype=F32)[:, None]
    w = 2.0 * math.pi * jnp.arange(length, dtype=F32)[:, None] / length
    f = jnp.linspace(1e-4, HY_BANDS - 1, HY_BANDS, dtype=F32)[None, :]
    emb = jnp.concatenate([t01, jnp.cos(f * w), -jnp.sin(f * w)], -1)[lag]
    mask = jnp.ones((n, 1), F32).at[length, 0].set(0.0)
    emb = jnp.concatenate([emb, jnp.zeros((n, LANES - HY_EMB - 1), F32), mask], -1)

    def pad2(a, r, c):
        return jnp.zeros((r, c), F32).at[:a.shape[0], :a.shape[1]].set(a)

    w1 = pad2(f_w1, LANES, LANES)
    b1 = pad2(f_b1[None], 1, LANES)
    fr1 = pad2(f_fr1[None], 1, LANES)
    w2 = pad2(f_w2, LANES, LANES)
    b2 = pad2(f_b2[None], 1, LANES)
    fr2 = pad2(f_fr2[None], 1, LANES)
    w3 = pad2(f_w3, LANES, f_w3.shape[1])
    deltas = jnp.abs(jnp.linspace(math.log(HY_TARGET) / HY_SLOW_DECAY, math.log(HY_TARGET) / HY_FAST_DECAY,
                                  d, dtype=F32))[None, :]
    tr = min(512, length)
    half_tiles = length // tr
    small = lambda shape: pl.BlockSpec(shape, lambda r, o: (0, 0))
    kern = pl.pallas_call(
        _hyena_filter_kernel, grid=(n // tr, HY_ORDER),
        in_specs=[pl.BlockSpec((tr, LANES), lambda r, o: (r, 0)),
                  small((LANES, LANES)), small((1, LANES)), small((1, LANES)),
                  small((LANES, LANES)), small((1, LANES)), small((1, LANES)),
                  pl.BlockSpec((LANES, d), lambda r, o: (0, 2 * o + r // half_tiles)),
                  small((1, d))],
        out_specs=pl.BlockSpec((tr, d), lambda r, o: (r, o)),
        out_shape=jax.ShapeDtypeStruct((n, HY_ORDER * d), F32),
        compiler_params=_params(("parallel", "parallel")))(emb, w1, b1, fr1, w2, b2, fr2, w3, deltas)
    c1, g, _, _ = mats
    ct = HY_C_TILE
    n_cols = HY_ORDER * d
    one = pl.Buffered(1)
    spec_r, spec_i = pl.pallas_call(
        functools.partial(_hyena_spec_kernel, n1=n1, n2=n2), grid=(n_cols // ct,),
        in_specs=[pl.BlockSpec((n, ct), lambda c: (0, c)),
                  pl.BlockSpec(c1.shape, lambda c: (0, 0), pipeline_mode=one),
                  pl.BlockSpec(g.shape, lambda c: (0, 0, 0), pipeline_mode=one)],
        out_specs=[pl.BlockSpec((n, ct), lambda c: (0, c)), pl.BlockSpec((n, ct), lambda c: (0, c))],
        out_shape=[jax.ShapeDtypeStruct((n, n_cols), F32)] * 2,
        scratch_shapes=[pltpu.VMEM((n1 * (2 * n2 + ROW_PAD), ct), F32)],
        compiler_params=_params(("parallel",)))(kern, c1, g)
    return spec_r, spec_i


def hyena_layer(cfg, xa, mod, w_in, b_in, conv_w, conv_b, f_w1, f_b1, f_fr1, f_w2, f_b2, f_fr2, f_w3,
                bias, w_out, b_out, lng, lnb, ctx_out):
    d = cfg.D
    n_ct = d // HY_C_TILE
    z = fused_linear(xa, w_in.astype(BF16), tm=cfg.tm, tn=3 * d // 2, out_defs=[(3 * d, 3 * d // 2, BF16)],
                     epilogue=lambda acc, rows, cols, m: [acc + cols[0]], seg_fn=_seg_fn(cfg), mod=mod,
                     mod_rows=(0, 1), col_extras=[b_in.reshape(1, -1)],
                     n_rows=None if ctx_out else cfg.NL)[0]
    conv_b2 = conv_b.reshape(1, -1)
    seqs = [(cfg.L, cfg.B, 0)]
    if ctx_out:
        seqs.append((cfg.CL, cfg.B, cfg.NL // cfg.CL))
    ys = []
    for length, n_seq, row_blk0 in seqs:
        mats = _fft_mats(*_fft_sizes(length))
        spec_r, spec_i = _hyena_spectra(length, d, f_w1, f_b1, f_fr1, f_w2, f_b2, f_fr2, f_w3, mats)
        y1 = _hyena_conv(None, z, 2 * n_ct, 0, conv_w, conv_b2, bias[0:1], spec_r, spec_i, 0,
                         mats, length, n_seq, row_blk0, F32)
        ys.append(_hyena_conv(y1, z, 0, n_ct, conv_w, conv_b2, bias[1:2], spec_r, spec_i, n_ct,
                              mats, length, n_seq, row_blk0, BF16))
    y = ys[0] if len(ys) == 1 else jnp.concatenate(ys, 0)
    n_rows = None if ctx_out else cfg.NL
    return residual_ln_linear(cfg, xa, y, w_out.astype(BF16), mod, 2, lng, lnb, bias=b_out, n_rows=n_rows)


def ada_modulation(cfg, cvec, ada_w, ada_b, layer):
    d = cfg.D
    out = fused_linear(cvec, ada_w, w_layer=layer, tm=cvec.shape[0], tn=d, out_defs=[(6 * d, d, F32)],
                       epilogue=lambda acc, rows, cols, m: [acc + cols[0]], pre=jax.nn.silu,
                       col_extras=[ada_b[layer].reshape(1, -1)])[0]
    return out[:cfg.B + 1].reshape(cfg.B + 1, 6, d)


def kernel(x, c, ctx, c_ctx, ada_w, ada_b, ln_g, ln_b, mla_w_dq, mla_q_norm, mla_w_uq, mla_w_dkv, mla_kv_norm,
           mla_w_ukv, mla_w_o, hy_w_in, hy_b_in, hy_conv_w, hy_conv_b, hy_f_w1, hy_f_b1, hy_f_fr1, hy_f_w2,
           hy_f_b2, hy_f_fr2, hy_f_w3, hy_bias, hy_w_out, hy_b_out, gqa_w_qkv, gqa_q_norm, gqa_k_norm, gqa_w_o,
           peer_w_q, peer_keys1, peer_keys2, peer_u, peer_v):
    batch, seq, d = x.shape
    cfg = Cfg(batch, seq, ctx.shape[1], d)
    xa = jnp.concatenate([x.reshape(-1, d), ctx.reshape(-1, d)], 0)
    pad_rows = -(batch + 1) % SUBLANES
    cvec = jnp.concatenate([c, c_ctx[None, :], jnp.zeros((pad_rows, d), F32)], 0)
    peer_w = peer_prepare(peer_w_q, peer_keys1, peer_keys2, peer_u, peer_v)
    ia = ib = ic = 0
    for layer in range(DEPTH):
        last = layer == DEPTH - 1
        mod = ada_modulation(cfg, cvec, ada_w, ada_b, layer)
        lng, lnb = ln_g[layer, 0], ln_b[layer, 0]
        kind = layer % N_MIXERS
        if kind == 0:
            xm = mla_layer(cfg, xa, mod, mla_w_dq[ia], mla_q_norm[ia], mla_w_uq[ia], mla_w_dkv[ia],
                           mla_kv_norm[ia], mla_w_ukv[ia], mla_w_o[ia], lng, lnb, not last)
            ia += 1
        elif kind == 1:
            xm = hyena_layer(cfg, xa, mod, hy_w_in[ib], hy_b_in[ib], hy_conv_w[ib], hy_conv_b[ib], hy_f_w1[ib],
                             hy_f_b1[ib], hy_f_fr1[ib], hy_f_w2[ib], hy_f_b2[ib], hy_f_fr2[ib], hy_f_w3[ib],
                             hy_bias[ib], hy_w_out[ib], hy_b_out[ib], lng, lnb, not last)
            ib += 1
        else:
            xm = gqa_layer(cfg, xa, mod, gqa_w_qkv[ic], gqa_q_norm[ic], gqa_k_norm[ic], gqa_w_o[ic],
                           lng, lnb, not last)
            ic += 1
        xa = peer_layer(cfg, xm, mod, layer, *peer_w, ln_g[layer, 1], ln_b[layer, 1], xm.shape[0])
    return xa[:cfg.NL].reshape(batch, seq, d)
```

```python
import functools
import math

import numpy as np
import jax
import jax.numpy as jnp
from jax import lax
from jax.experimental import pallas as pl
from jax.experimental.pallas import tpu as pltpu

F32 = jnp.float32
BF16 = jnp.bfloat16

DEPTH = 4
GRID_W = 64
N_MIXERS = 3
DEEPNORM_ALPHA = (2 * DEPTH) ** 0.25
LN_EPS = 1e-5
RMS_EPS = 1e-6
ROPE_THETA = 10000.0
MLA_HEADS = 8
MLA_Q_RANK = 768
MLA_KV_RANK = 256
MLA_NOPE = 128
MLA_ROPE = 64
MLA_V = 128
HY_ORDER = 2
HY_EMB = 33
HY_BANDS = (HY_EMB - 1) // 2
HY_SHORT = 3
HY_FAST_DECAY = 0.3
HY_SLOW_DECAY = 1.5
HY_TARGET = 1e-2
HY_SHIFT = 0.05
GQA_Q_HEADS = 8
GQA_KV_HEADS = 2
GQA_HEAD_DIM = 128
PEER_HEADS = 8
PEER_N_KEYS = 128
PEER_TOPK = 16
PEER_D_KEY = 256

LANES = 128
SUBLANES = 8
VMEM_LIMIT = 56 * 1024 * 1024


def _params(sem):
    return pltpu.CompilerParams(dimension_semantics=sem, vmem_limit_bytes=VMEM_LIMIT)


class Cfg:
    def __init__(self, batch, seq, ctx_len, d):
        self.B, self.L, self.CL, self.D = batch, seq, ctx_len, d
        self.NL = batch * seq
        self.NC = batch * ctx_len
        self.NT = self.NL + self.NC
        self.tm = math.gcd(512, math.gcd(seq, self.NC))


def _linear_kernel(*refs, has_mod, mod_rows, n_row, n_col, n_out, pre, epilogue):
    it = iter(refs)
    x_ref = next(it)
    mod_ref = next(it) if has_mod else None
    w_ref = next(it)
    row_refs = [next(it) for _ in range(n_row)]
    col_refs = [next(it) for _ in range(n_col)]
    out_refs = [next(it) for _ in range(n_out)]
    x = x_ref[...]
    mod = mod_ref[...] if has_mod else None
    if pre is not None:
        x = pre(x)
    if mod_rows is not None:
        sh, sc = mod_rows
        x = x * (1.0 + mod[sc:sc + 1, :]) + mod[sh:sh + 1, :]
    acc = jnp.dot(x.astype(BF16), w_ref[...].astype(BF16), preferred_element_type=F32)
    outs = epilogue(acc, [r[...] for r in row_refs], [c[...] for c in col_refs], mod)
    for o_ref, o in zip(out_refs, outs):
        o_ref[...] = o.astype(o_ref.dtype)


def fused_linear(x, w, *, tm, tn, out_defs, epilogue, seg_fn=None, mod=None, mod_rows=None,
                 w_layer=None, row_extras=(), col_extras=(), pre=None, n_rows=None):
    rows, k = x.shape
    if n_rows is not None:
        rows = n_rows
    n = w.shape[-1]
    grid = (rows // tm, n // tn)
    in_specs = [pl.BlockSpec((tm, k), lambda i, j: (i, 0))]
    args = [x]
    if mod is not None:
        in_specs.append(pl.BlockSpec((None,) + mod.shape[1:], lambda i, j: (seg_fn(i), 0, 0)))
        args.append(mod)
    if w_layer is None:
        in_specs.append(pl.BlockSpec((k, tn), lambda i, j: (0, j)))
    else:
        in_specs.append(pl.BlockSpec((None, k, tn), lambda i, j: (w_layer, 0, j)))
    args.append(w)
    for r in row_extras:
        in_specs.append(pl.BlockSpec((tm, r.shape[1]), lambda i, j: (i, 0)))
        args.append(r)
    for c in col_extras:
        in_specs.append(pl.BlockSpec((c.shape[0], tn), lambda i, j: (0, j)))
        args.append(c)
    out_specs = [pl.BlockSpec((tm, bc), lambda i, j: (i, j)) for (_, bc, _) in out_defs]
    out_shape = [jax.ShapeDtypeStruct((rows, tc), dt) for (tc, _, dt) in out_defs]
    kern = functools.partial(_linear_kernel, has_mod=mod is not None, mod_rows=mod_rows,
                             n_row=len(row_extras), n_col=len(col_extras), n_out=len(out_defs),
                             pre=pre, epilogue=epilogue)
    outs = pl.pallas_call(kern, grid=grid, in_specs=in_specs, out_specs=out_specs, out_shape=out_shape,
                          compiler_params=_params(("parallel", "parallel")))(*args)
    return outs


def _linear_t_kernel(*refs, has_mod, mod_rows):
    if has_mod:
        x_ref, mod_ref, w_ref, o_ref = refs
        mod = mod_ref[...]
        sh, sc = mod_rows
        x = (x_ref[...] * (1.0 + mod[sc:sc + 1, :]) + mod[sh:sh + 1, :]).astype(BF16)
    else:
        x_ref, w_ref, o_ref = refs
        x = x_ref[...].astype(BF16)
    o_ref[...] = lax.dot_general(w_ref[...], x, (((0,), (1,)), ((), ())),
                                 preferred_element_type=F32).astype(o_ref.dtype)


def linear_transposed(x, w, *, tm, seg_fn=None, mod=None, mod_rows=None):
    rows, k = x.shape
    n = w.shape[1]
    in_specs = [pl.BlockSpec((tm, k), lambda i: (i, 0))]
    args = [x]
    if mod is not None:
        in_specs.append(pl.BlockSpec((None,) + mod.shape[1:], lambda i: (seg_fn(i), 0, 0)))
        args.append(mod)
    in_specs.append(pl.BlockSpec((k, n), lambda i: (0, 0)))
    args.append(w)
    return pl.pallas_call(
        functools.partial(_linear_t_kernel, has_mod=mod is not None, mod_rows=mod_rows),
        grid=(rows // tm,), in_specs=in_specs, out_specs=pl.BlockSpec((n, tm), lambda i: (0, i)),
        out_shape=jax.ShapeDtypeStruct((n, rows), BF16), compiler_params=_params(("parallel",)))(*args)


def _rms(y, gain):
    return y * lax.rsqrt(jnp.mean(jnp.square(y), axis=-1, keepdims=True) + RMS_EPS) * gain


def _rope(y, cos, sin):
    return y * cos + pltpu.roll(y, LANES // 2, 1) * sin


def _layer_norm(v, g, b):
    mu = jnp.mean(v, axis=-1, keepdims=True)
    vc = v - mu
    var = jnp.mean(jnp.square(vc), axis=-1, keepdims=True)
    return vc * lax.rsqrt(var + LN_EPS) * g + b


def residual_ln_linear(cfg, x, y, w, mod, gate_row, lng, lnb, bias=None, n_rows=None):
    d = cfg.D

    def epi(acc, rows, cols, m):
        if bias is not None:
            acc = acc + cols[2]
        v = DEEPNORM_ALPHA * rows[0] + m[gate_row:gate_row + 1, :] * acc
        return [_layer_norm(v, cols[0], cols[1])]

    cols = [lng.reshape(1, d), lnb.reshape(1, d)] + ([bias.reshape(1, d)] if bias is not None else [])
    return fused_linear(y, w, tm=cfg.tm, tn=d, out_defs=[(d, d, F32)], epilogue=epi,
                        seg_fn=_seg_fn(cfg), mod=mod, row_extras=[x], col_extras=cols, n_rows=n_rows)[0]


def _seg_fn(cfg):
    tiles_per_batch = cfg.L // cfg.tm
    nb = cfg.B
    return lambda i: jnp.minimum(i // tiles_per_batch, nb)


ATTN_KEY_CHUNK = 256
ATTN_CHUNK_UNROLL = 8


def _attn_kernel(*refs, n_parts, n_segs, heads, tq):
    q_ref = refs[0]
    pos = 1
    segs = []
    for _ in range(n_segs):
        segs.append((refs[pos:pos + n_parts], refs[pos + n_parts]))
        pos += n_parts + 1
    o_ref, sc_scr, acc_scr = refs[pos], refs[pos + 1], refs[pos + 2]
    width = n_parts * LANES
    qv = q_ref[...]
    q = qv if heads == 1 else jnp.concatenate([qv[:, i * width:(i + 1) * width] for i in range(heads)], 0)
    m = None
    row0 = 0
    for k_refs, _ in segs:
        k = k_refs[0][...] if n_parts == 1 else jnp.concatenate([r[...] for r in k_refs], 1)
        sc = lax.dot_general(k, q, (((1,), (1,)), ((), ())), preferred_element_type=F32)
        sc_scr[pl.ds(row0, sc.shape[0]), :] = sc
        row0 += sc.shape[0]
        ms = jnp.max(sc, axis=0, keepdims=True)
        m = ms if m is None else jnp.maximum(m, ms)
    ones = jnp.ones((2 * SUBLANES, ATTN_KEY_CHUNK), BF16)
    acc_scr[...] = jnp.zeros_like(acc_scr)
    row0 = 0
    for _, vt_ref in segs:
        def chunk(c, carry, vt_ref=vt_ref, row0=row0):
            keys = pl.ds(pl.multiple_of(row0 + c * ATTN_KEY_CHUNK, ATTN_KEY_CHUNK), ATTN_KEY_CHUNK)
            cols = pl.ds(pl.multiple_of(c * ATTN_KEY_CHUNK, ATTN_KEY_CHUNK), ATTN_KEY_CHUNK)
            p = jnp.exp((sc_scr[keys, :] - m).astype(BF16))
            vt = jnp.concatenate([vt_ref[:, cols], ones], 0)
            acc_scr[...] += jnp.dot(vt, p, preferred_element_type=F32)
            return carry

        n_chunks = vt_ref.shape[1] // ATTN_KEY_CHUNK
        lax.fori_loop(0, n_chunks, chunk, 0, unroll=min(n_chunks, ATTN_CHUNK_UNROLL))
        row0 += vt_ref.shape[1]
    acc = acc_scr[...]
    o = (acc[:LANES] / acc[LANES:LANES + 1]).T
    if heads > 1:
        o = jnp.concatenate([o[i * tq:(i + 1) * tq] for i in range(heads)], 1)
    o_ref[...] = o.astype(o_ref.dtype)


def attention(q_arr, q_rows, out_rows, key_segs, *, n_parts, heads, n_batch, n_groups, n_qt, tq, out_rows_total):
    qw = heads * n_parts * LANES
    in_specs = [pl.BlockSpec((tq, qw), lambda b, g, t: (q_rows(b, t), g))]
    args = [q_arr]
    for keys_blk, blk_fn, k_parts, (vt_arr, vt_row) in key_segs:
        for k_arr, k_col in k_parts:
            in_specs.append(pl.BlockSpec((keys_blk, LANES),
                                         lambda b, g, t, blk_fn=blk_fn, k_col=k_col: (blk_fn(b), k_col(g))))
            args.append(k_arr)
        in_specs.append(pl.BlockSpec((LANES, keys_blk),
                                     lambda b, g, t, blk_fn=blk_fn, vt_row=vt_row: (vt_row(g), blk_fn(b))))
        args.append(vt_arr)
    out_spec = pl.BlockSpec((tq, heads * LANES), lambda b, g, t: (out_rows(b, t), g))
    kern = functools.partial(_attn_kernel, n_parts=n_parts, n_segs=len(key_segs), heads=heads, tq=tq)
    total_keys = sum(seg[0] for seg in key_segs)
    return pl.pallas_call(
        kern, grid=(n_batch, n_groups, n_qt), in_specs=in_specs, out_specs=out_spec,
        scratch_shapes=[pltpu.VMEM((total_keys, heads * tq), F32),
                        pltpu.VMEM((LANES + 2 * SUBLANES, heads * tq), F32)],
        out_shape=jax.ShapeDtypeStruct((out_rows_total, n_groups * heads * LANES), BF16),
        compiler_params=_params(("parallel", "parallel", "arbitrary")))(*args)


def _attend_both(cfg, q_arr, k_parts, vt_part, n_parts, heads, n_groups, tq, ctx_out):
    tq = min(tq, cfg.L)
    n_qt = cfg.L // tq
    lat_blocks_ctx = cfg.NL // cfg.CL
    seg_lat = (cfg.L, lambda b: b, k_parts, vt_part)
    seg_ctx = (cfg.CL, lambda b: lat_blocks_ctx + b, k_parts, vt_part)
    lat_rows = lambda b, t: b * n_qt + t
    y_lat = attention(q_arr, lat_rows, lat_rows, [seg_ctx, seg_lat], n_parts=n_parts, heads=heads,
                      n_batch=cfg.B, n_groups=n_groups, n_qt=n_qt, tq=tq, out_rows_total=cfg.NL)
    if not ctx_out:
        return y_lat
    y_ctx = attention(q_arr, lambda b, t: lat_blocks_ctx + b, lambda b, t: b, [seg_ctx], n_parts=n_parts,
                      heads=heads, n_batch=cfg.B, n_groups=n_groups, n_qt=1, tq=cfg.CL, out_rows_total=cfg.NC)
    return jnp.concatenate([y_lat, y_ctx], axis=0)


def _rope_tables(cfg, rot_dim, half_width):
    rows = cfg.L // GRID_W
    row = jnp.repeat(jnp.arange(rows, dtype=F32), GRID_W)
    col = jnp.tile(jnp.arange(GRID_W, dtype=F32), rows)
    quarter = rot_dim // 4
    inv_freq = ROPE_THETA ** (-jnp.arange(quarter, dtype=F32) / quarter)
    ang = jnp.concatenate([row[:, None] * inv_freq, col[:, None] * inv_freq], -1)
    c, s = jnp.cos(ang), jnp.sin(ang)
    pad = half_width - rot_dim // 2
    one = jnp.ones((cfg.L, pad), F32)
    zero = jnp.zeros((cfg.L, pad), F32)
    cos_t = jnp.concatenate([c, one, c, one], -1)
    sin_t = jnp.concatenate([-s, zero, s, zero], -1)
    cos_t = jnp.concatenate([jnp.tile(cos_t, (cfg.B, 1)), jnp.ones((cfg.NC, LANES), F32)], 0)
    sin_t = jnp.concatenate([jnp.tile(sin_t, (cfg.B, 1)), jnp.zeros((cfg.NC, LANES), F32)], 0)
    return cos_t, sin_t


def _deinterleave_cols(w, rot_dim):
    half = rot_dim // 2
    pad = LANES // 2 - half
    ev = w[..., 0::2]
    od = w[..., 1::2]
    z = jnp.zeros(w.shape[:-1] + (pad,), w.dtype)
    return jnp.concatenate([ev, z, od, z], -1)


def mla_layer(cfg, xa, mod, w_dq, q_norm, w_uq, w_dkv, kv_norm, w_ukv, w_o, lng, lnb, ctx_out):
    tm = cfg.tm
    seg = _seg_fn(cfg)
    scale = (MLA_NOPE + MLA_ROPE) ** -0.5
    cos_t, sin_t = _rope_tables(cfg, MLA_ROPE, LANES // 2)
    hq = MLA_HEADS

    w_uq3 = w_uq.reshape(MLA_Q_RANK, hq, MLA_NOPE + MLA_ROPE)
    w_uq_p = jnp.concatenate([w_uq3[..., :MLA_NOPE], _deinterleave_cols(w_uq3[..., MLA_NOPE:], MLA_ROPE)], -1)
    w_uq_p = w_uq_p.reshape(MLA_Q_RANK, hq * 2 * LANES).astype(BF16)
    w_dkv_p = jnp.concatenate([w_dkv[:, :MLA_KV_RANK], _deinterleave_cols(w_dkv[:, MLA_KV_RANK:], MLA_ROPE)],
                              -1).astype(BF16)

    qr, kr = MLA_Q_RANK, MLA_KV_RANK

    def epi_down(acc, rows, cols, m):
        g = cols[0]
        return [_rms(acc[:, :qr], g[:, :qr]),
                _rms(acc[:, qr:qr + kr], g[:, qr:qr + kr]),
                _rope(acc[:, qr + kr:], rows[0], rows[1])]

    w_down = jnp.concatenate([w_dq.astype(BF16), w_dkv_p], -1)
    gains = jnp.concatenate([q_norm, kv_norm, jnp.ones((LANES,), F32)]).reshape(1, -1)
    n_down = qr + kr + LANES
    cq, ckv, kpe = fused_linear(xa, w_down, tm=tm, tn=n_down,
                                out_defs=[(qr, qr, BF16), (kr, kr, BF16), (LANES, LANES, BF16)],
                                epilogue=epi_down, seg_fn=seg, mod=mod, mod_rows=(0, 1),
                                row_extras=[cos_t, sin_t], col_extras=[gains])

    heads_per_tile = 4

    def epi_q(acc, rows, cols, m):
        parts = []
        for hh in range(heads_per_tile):
            c0 = hh * 2 * LANES
            parts.append(acc[:, c0:c0 + LANES] * scale)
            parts.append(_rope(acc[:, c0 + LANES:c0 + 2 * LANES] * scale, rows[0], rows[1]))
        return [jnp.concatenate(parts, -1)]

    tn_q = heads_per_tile * 2 * LANES
    q = fused_linear(cq, w_uq_p, tm=tm, tn=tn_q, out_defs=[(hq * 2 * LANES, tn_q, BF16)],
                     epilogue=epi_q, row_extras=[cos_t, sin_t])[0]

    w_ukv3 = w_ukv.astype(BF16).reshape(kr, hq, MLA_NOPE + MLA_V)
    w_uk = w_ukv3[:, :, :MLA_NOPE].reshape(kr, hq * MLA_NOPE)
    w_uv = w_ukv3[:, :, MLA_NOPE:].reshape(kr, hq * MLA_V)
    kn = fused_linear(ckv, w_uk, tm=tm, tn=hq * MLA_NOPE, out_defs=[(hq * MLA_NOPE, hq * MLA_NOPE, BF16)],
                      epilogue=lambda acc, rows, cols, m: [acc])[0]
    vt = linear_transposed(ckv, w_uv, tm=tm)
    y = _attend_both(cfg, q, [(kn, lambda h: h), (kpe, lambda h: 0)], (vt, lambda h: h),
                     n_parts=2, heads=1, n_groups=hq, tq=512, ctx_out=ctx_out)
    n_rows = None if ctx_out else cfg.NL
    return residual_ln_linear(cfg, xa, y, w_o.astype(BF16), mod, 2, lng, lnb, n_rows=n_rows)


def gqa_layer(cfg, xa, mod, w_qkv, q_norm, k_norm, w_o, lng, lnb, ctx_out):
    tm = cfg.tm
    seg = _seg_fn(cfg)
    hd = GQA_HEAD_DIM
    nq, nk = GQA_Q_HEADS, GQA_KV_HEADS
    scale = hd ** -0.5
    cos_t, sin_t = _rope_tables(cfg, hd, LANES // 2)
    perm = np.concatenate([np.arange(0, hd, 2), np.arange(1, hd, 2)])
    w3 = w_qkv.reshape(cfg.D, nq + 2 * nk, hd)
    w_qk = w3[:, :nq + nk, perm].reshape(cfg.D, -1).astype(BF16)
    w_v = w3[:, nq + nk:, :].reshape(cfg.D, -1).astype(BF16)
    gains = jnp.stack([q_norm[perm], k_norm[perm]])
    n_chunks = nq + nk

    def epi(acc, rows, cols, m):
        g = cols[0]
        outs = []
        for c in range(n_chunks):
            y = acc[:, c * hd:(c + 1) * hd]
            if c < nq:
                y = _rope(_rms(y, g[0:1, :hd]) * scale, rows[0], rows[1])
            else:
                y = _rope(_rms(y, g[1:2, :hd]), rows[0], rows[1])
            outs.append(y)
        return [jnp.concatenate(outs, -1)]

    n_out = n_chunks * hd
    gains_full = jnp.tile(gains, (1, n_chunks))
    qkv = fused_linear(xa, w_qk, tm=tm, tn=n_out, out_defs=[(n_out, n_out, BF16)], epilogue=epi,
                       seg_fn=seg, mod=mod, mod_rows=(0, 1), row_extras=[cos_t, sin_t],
                       col_extras=[gains_full])[0]
    grp = nq // nk
    vt = linear_transposed(xa, w_v, tm=tm, seg_fn=seg, mod=mod, mod_rows=(0, 1))
    y = _attend_both(cfg, qkv, [(qkv, lambda g: nq + g)], (vt, lambda g: g),
                     n_parts=1, heads=grp, n_groups=nk, tq=128, ctx_out=ctx_out)
    n_rows = None if ctx_out else cfg.NL
    return residual_ln_linear(cfg, xa, y, w_o.astype(BF16), mod, 2, lng, lnb, n_rows=n_rows)


PEER_E1_PER_STEP = 16
PEER_E1_PER_DOT = 8
GATE_STEP = 128.0
_PAIRS = [(a, b) for a in range(PEER_TOPK) for b in range(PEER_TOPK) if (a + 1) * (b + 1) <= PEER_TOPK]


def _insert_sorted(lst, x):
    out = []
    for a in lst:
        out.append(jnp.maximum(a, x))
        x = jnp.minimum(a, x)
    return out


def _gelu(x):
    return 0.5 * x * (1.0 + lax.erf(x * (2.0 ** -0.5)))


def _peer_route_group(s1_scr, s2_scr, cnt_scr, r2_scr, g):
    nk = PEER_N_KEYS
    k = PEER_TOPK
    neg = jnp.full((PEER_HEADS, LANES), -jnp.inf, F32)

    def key_rows(i):
        return pl.ds(pl.multiple_of(i * PEER_HEADS, PEER_HEADS), PEER_HEADS)

    def top_values(load):
        return lax.fori_loop(0, nk, lambda i, lst: tuple(_insert_sorted(lst, load(i))), (neg,) * k, unroll=4)

    v1 = top_values(lambda i: s1_scr[g, key_rows(i), :])
    v2 = top_values(lambda i: s2_scr[g, key_rows(i), :])
    cand = {ab: v1[ab[0]] + v2[ab[1]] for ab in _PAIRS}
    top = [neg] * k
    for ab in _PAIRS:
        top = _insert_sorted(top, cand[ab])
    tau = top[k - 1]
    cmax = cand[(0, 0)]
    zsum = jnp.zeros_like(tau)
    for ab in _PAIRS:
        zsum = zsum + jnp.where(cand[ab] >= tau, jnp.exp(cand[ab] - cmax), 0.0)
    inv_z = 1.0 / zsum
    m1, m2 = v1[0], v2[0]

    def finish(i, carry):
        rows = key_rows(i)
        x1 = s1_scr[g, rows, :]
        x2 = s2_scr[g, rows, :]
        cnt = jnp.zeros_like(x1)
        r2 = jnp.zeros_like(x2)
        for b in range(k):
            cnt = cnt + jnp.where(x1 + v2[b] >= tau, 1.0, 0.0)
            r2 = r2 + jnp.where(v2[b] > x2, 1.0, 0.0)
        cnt_scr[g, rows, :] = cnt * GATE_STEP
        r2_scr[g, rows, :] = r2 * GATE_STEP
        s1_scr[g, rows, :] = jnp.exp(x1 - m1) * inv_z
        s2_scr[g, rows, :] = jnp.exp(x2 - m2)
        return carry

    lax.fori_loop(0, nk, finish, 0, unroll=2)


def _peer_kernel(x_ref, mod_ref, wq_ref, k1_ref, k2_ref, *rest, tokens):
    n_sub = PEER_E1_PER_STEP // PEER_E1_PER_DOT
    u_refs, vt_refs = rest[:n_sub], rest[n_sub:2 * n_sub]
    (lng_ref, lnb_ref, o_ref, ht_scr, s1_scr, s2_scr, cnt_scr, r2_scr, r2b_scr, p2b_scr, act_scr,
     acc_scr) = rest[2 * n_sub:]
    e = pl.program_id(1)
    n_groups = tokens // LANES
    nh = PEER_HEADS
    nk = PEER_N_KEYS
    pack = 2 * SUBLANES

    @pl.when(e == 0)
    def _route():
        mod = mod_ref[...]
        h = x_ref[...] * (1.0 + mod[4:5, :]) + mod[3:4, :]
        ht = h.T.astype(BF16)
        ht_scr[...] = ht
        qt = jnp.dot(wq_ref[...], ht, preferred_element_type=F32)
        half_rows = nh * (PEER_D_KEY // 2)
        s1 = jnp.dot(k1_ref[...], qt[:half_rows].astype(BF16), preferred_element_type=F32)
        s2 = jnp.dot(k2_ref[...], qt[half_rows:].astype(BF16), preferred_element_type=F32)
        for g in range(n_groups):
            s1_scr[g] = s1[:, g * LANES:(g + 1) * LANES]
            s2_scr[g] = s2[:, g * LANES:(g + 1) * LANES]
        for g in range(n_groups):
            _peer_route_group(s1_scr, s2_scr, cnt_scr, r2_scr, g)
        for g in range(n_groups):
            for hd in range(nh):
                r2b_scr[g, hd] = r2_scr[g, pl.ds(hd, nk, stride=nh), :].astype(BF16)
                p2b_scr[g, hd] = s2_scr[g, pl.ds(hd, nk, stride=nh), :].astype(BF16)
        acc_scr[...] = jnp.zeros_like(acc_scr)

    def up_projection(k):
        act = jnp.dot(u_refs[k][...], ht_scr[...], preferred_element_type=F32)
        for g in range(n_groups):
            act_scr[k % 2, g] = act[:, g * LANES:(g + 1) * LANES]

    up_projection(0)
    for k in range(n_sub):
        if k + 1 < n_sub:
            up_projection(k + 1)
        row0 = pl.multiple_of((e * PEER_E1_PER_STEP + k * PEER_E1_PER_DOT) * nh, nh)
        w_cols = []
        for g in range(n_groups):
            cnt = cnt_scr[g, pl.ds(row0, PEER_E1_PER_DOT * nh), :]
            p1 = s1_scr[g, pl.ds(row0, PEER_E1_PER_DOT * nh), :]
            gates = [jnp.zeros((nk // pack, pack, LANES), BF16) for _ in range(PEER_E1_PER_DOT)]
            for hd in range(nh):
                r2t = r2b_scr[g, hd].reshape(nk // pack, pack, LANES)
                p2t = p2b_scr[g, hd].reshape(nk // pack, pack, LANES)
                for jj in range(PEER_E1_PER_DOT):
                    r = jj * nh + hd
                    cb = jnp.broadcast_to(cnt[r:r + 1, :], (pack, LANES)).astype(BF16)
                    pb = jnp.broadcast_to(p1[r:r + 1, :], (pack, LANES)).astype(BF16)
                    gates[jj] = gates[jj] + jnp.minimum(jnp.maximum(cb[None] - r2t, 0), pb[None]) * p2t
            w_parts = []
            for jj in range(PEER_E1_PER_DOT):
                a = _gelu(act_scr[k % 2, g, pl.ds(jj * nk, nk), :].astype(BF16))
                w_parts.append(gates[jj].reshape(nk, LANES) * a)
            w_cols.append(jnp.concatenate(w_parts, axis=0))
        w = jnp.concatenate(w_cols, axis=1)
        acc_scr[...] += lax.dot_general(vt_refs[k][...], w, (((0,), (0,)), ((), ())),
                                        preferred_element_type=F32)

    @pl.when(e == pl.num_programs(1) - 1)
    def _finish():
        mod = mod_ref[...]
        f = acc_scr[...].T
        v = DEEPNORM_ALPHA * x_ref[...] + mod[5:6, :] * f
        o_ref[...] = _layer_norm(v, lng_ref[...], lnb_ref[...])


def peer_layer(cfg, xa, mod, layer, wq_t, k1_bd, k2_bd, u_b, vt_b, lng, lnb, n_rows):
    d = cfg.D
    t = cfg.tm
    n_exp = u_b.shape[1]
    e_tile = PEER_E1_PER_STEP * PEER_N_KEYS
    seg = _seg_fn(cfg)
    nq = wq_t.shape[1]
    nkh = k1_bd.shape[1]
    ng = t // LANES
    in_specs = [
        pl.BlockSpec((t, d), lambda i, e: (i, 0)),
        pl.BlockSpec((None, 6, d), lambda i, e: (seg(i), 0, 0)),
        pl.BlockSpec((None, nq, d), lambda i, e: (layer, 0, 0), pipeline_mode=pl.Buffered(1)),
        pl.BlockSpec((None, nkh, nkh), lambda i, e: (layer, 0, 0), pipeline_mode=pl.Buffered(1)),
        pl.BlockSpec((None, nkh, nkh), lambda i, e: (layer, 0, 0), pipeline_mode=pl.Buffered(1)),
    ]
    n_sub = PEER_E1_PER_STEP // PEER_E1_PER_DOT
    sub_rows = PEER_E1_PER_DOT * PEER_N_KEYS
    in_specs += [pl.BlockSpec((None, sub_rows, d), lambda i, e, k=k: (layer, e * n_sub + k, 0))
                 for k in range(n_sub)]
    in_specs += [pl.BlockSpec((None, sub_rows, d), lambda i, e, k=k: (layer, e * n_sub + k, 0))
                 for k in range(n_sub)]
    in_specs += [
        pl.BlockSpec((1, d), lambda i, e: (0, 0)),
        pl.BlockSpec((1, d), lambda i, e: (0, 0)),
    ]
    scratch = [
        pltpu.VMEM((d, t), BF16),
        pltpu.VMEM((ng, nkh, LANES), F32),
        pltpu.VMEM((ng, nkh, LANES), F32),
        pltpu.VMEM((ng, nkh, LANES), F32),
        pltpu.VMEM((ng, nkh, LANES), F32),
        pltpu.VMEM((ng, PEER_HEADS, PEER_N_KEYS, LANES), BF16),
        pltpu.VMEM((ng, PEER_HEADS, PEER_N_KEYS, LANES), BF16),
        pltpu.VMEM((2, ng, PEER_E1_PER_DOT * PEER_N_KEYS, LANES), F32),
        pltpu.VMEM((d, t), F32),
    ]
    return pl.pallas_call(
        functools.partial(_peer_kernel, tokens=t),
        grid=(n_rows // t, n_exp // e_tile), in_specs=in_specs,
        out_specs=pl.BlockSpec((t, d), lambda i, e: (i, 0)),
        out_shape=jax.ShapeDtypeStruct((n_rows, d), F32), scratch_shapes=scratch,
        compiler_params=_params(("parallel", "arbitrary")))(
            xa, mod, wq_t, k1_bd, k2_bd, *([u_b] * n_sub), *([vt_b] * n_sub), lng.reshape(1, d), lnb.reshape(1, d))


def peer_prepare(peer_w_q, peer_keys1, peer_keys2, peer_u, peer_v):
    depth, d, _ = peer_w_q.shape
    half = PEER_D_KEY // 2
    nh = PEER_HEADS
    wq_t = peer_w_q.reshape(depth, d, nh, 2, half).transpose(0, 3, 2, 4, 1).reshape(depth, 2 * nh * half, d)
    eye = jnp.eye(nh, dtype=F32)

    def bd(keys):
        return jnp.einsum('lhkd,hg->lkhgd', keys, eye).reshape(depth, PEER_N_KEYS * nh, nh * half).astype(BF16)

    return (wq_t.astype(BF16), bd(peer_keys1), bd(peer_keys2), peer_u.astype(BF16), peer_v.astype(BF16))


HY_C_TILE = LANES
ROW_PAD = SUBLANES
FFT_UNROLL = 32


def _fft_sizes(length):
    n = 2 * length
    n1 = 64 if n >= 8192 else (32 if n >= 1024 else 16)
    return n1, n // n1


def _fft_mats(n1, n2):
    n = n1 * n2
    two_pi = 2.0 * math.pi
    k2 = jnp.arange(n2, dtype=jnp.int32)
    i1 = jnp.arange(n1, dtype=jnp.int32)
    ang = two_pi * ((k2[:, None] * k2[None, :]) % n2).astype(F32) / n2
    c1 = jnp.concatenate([jnp.cos(ang), -jnp.sin(ang)], 0)
    m = (i1[None, None, :] * k2[:, None, None] + i1[None, None, :] * i1[None, :, None] * n2) % n
    ang = two_pi * m.astype(F32) / n
    gr, gi = jnp.cos(ang), -jnp.sin(ang)
    g = jnp.concatenate([jnp.concatenate([gr, -gi], -1), jnp.concatenate([gi, gr], -1)], -2)
    hr, hi = jnp.swapaxes(gr, 1, 2), -jnp.swapaxes(gi, 1, 2)
    h = jnp.concatenate([jnp.concatenate([hr, -hi], -1), jnp.concatenate([hi, hr], -1)], -2)
    ang = two_pi * ((k2[:n2 // 2, None] * k2[None, :]) % n2).astype(F32) / n2
    c2 = jnp.concatenate([jnp.cos(ang), -jnp.sin(ang)], 1) / n
    return c1.astype(BF16), g.astype(BF16), h.astype(BF16), c2.astype(BF16)


def _fft_stage1(src, c1, a_scr, n1, n2, n2_in):
    pitch = 2 * n2 + ROW_PAD

    def body(i, carry):
        xs = src[pl.ds(i, n2_in, stride=n1), :].astype(BF16)
        a_scr[pl.ds(pl.multiple_of(i * pitch, SUBLANES), 2 * n2), :] = jnp.dot(c1, xs, preferred_element_type=F32)
        return carry

    lax.fori_loop(0, n1, body, 0, unroll=min(FFT_UNROLL, n1))


def _fft_stage2_load(a_scr, k, n1, n2):
    pitch = 2 * n2 + ROW_PAD
    ar = a_scr[pl.ds(k, n1, stride=pitch), :]
    ai = a_scr[pl.ds(n2 + k, n1, stride=pitch), :]
    return jnp.concatenate([ar, ai], 0).astype(BF16)


def _short_conv(z, w, b):
    rows = z.shape[0]
    idx = lax.broadcasted_iota(jnp.int32, z.shape, 0)
    prev = jnp.where(idx == 0, 0.0, pltpu.roll(z, 1, 0))
    nxt = jnp.where(idx == rows - 1, 0.0, pltpu.roll(z, rows - 1, 0))
    return b + prev * w[0:1, :] + z * w[1:2, :] + nxt * w[2:3, :]


def _hyena_conv_kernel(*refs, first, n1, n2):
    if first:
        (zu_ref, cw_u_ref, cb_u_ref, zx_ref, cw_x_ref, cb_x_ref, bias_ref, sr_ref, si_ref,
         c1_ref, g_ref, h_ref, c2_ref, o_ref, u_scr, y_scr, a_scr, b_scr) = refs
        u_scr[...] = _short_conv(zu_ref[...].astype(F32), cw_u_ref[...], cb_u_ref[...])
    else:
        (zu_ref, zx_ref, cw_x_ref, cb_x_ref, bias_ref, sr_ref, si_ref,
         c1_ref, g_ref, h_ref, c2_ref, o_ref, u_scr, y_scr, a_scr, b_scr) = refs
        u_scr[...] = zu_ref[...]
    _fft_stage1(u_scr, c1_ref[...], a_scr, n1, n2, n2 // 2)
    pitch_b = 2 * n1 + ROW_PAD

    def mid(k, carry):
        x = jnp.dot(g_ref[k], _fft_stage2_load(a_scr, k, n1, n2), preferred_element_type=F32)
        rows = pl.ds(pl.multiple_of(k * n1, n1), n1)
        xr, xi = x[:n1], x[n1:]
        sr, si = sr_ref[rows, :], si_ref[rows, :]
        y = jnp.concatenate([xr * sr - xi * si, xr * si + xi * sr], 0).astype(BF16)
        b_scr[pl.ds(pl.multiple_of(k * pitch_b, SUBLANES), 2 * n1), :] = jnp.dot(
            h_ref[k], y, preferred_element_type=F32)
        return carry

    lax.fori_loop(0, n2, mid, 0, unroll=min(FFT_UNROLL, n2))

    def last(i, carry):
        br = b_scr[pl.ds(i, n2, stride=pitch_b), :]
        bi = b_scr[pl.ds(n1 + i, n2, stride=pitch_b), :]
        y = jnp.dot(c2_ref[...], jnp.concatenate([br, bi], 0).astype(BF16), preferred_element_type=F32)
        y_scr[pl.ds(i, n2 // 2, stride=n1), :] = y
        return carry

    lax.fori_loop(0, n1, last, 0, unroll=min(FFT_UNROLL, n1))
    x = _short_conv(zx_ref[...].astype(F32), cw_x_ref[...], cb_x_ref[...])
    u = u_scr[...]
    o_ref[...] = (x * (y_scr[...] + u * bias_ref[...])).astype(o_ref.dtype)


def _hyena_conv(y_prev, z, zu_col0, zx_col0, conv_w, conv_b, bias_row, spec_r, spec_i, spec_col0, mats,
                length, n_seq, row_blk0, out_dtype):
    n1, n2 = _fft_sizes(length)
    c1, g, h, c2 = mats
    c1 = c1[:, :n2 // 2]
    ct = HY_C_TILE
    d = bias_row.shape[1]
    n = 2 * length
    one = pl.Buffered(1)
    first = y_prev is None

    def z_specs(col0):
        return [pl.BlockSpec((length, ct), lambda c, s: (row_blk0 + s, col0 + c)),
                pl.BlockSpec((HY_SHORT, ct), lambda c, s: (0, col0 + c)),
                pl.BlockSpec((1, ct), lambda c, s: (0, col0 + c))]

    if first:
        in_specs = z_specs(zu_col0)
        args = [z, conv_w, conv_b]
    else:
        in_specs = [pl.BlockSpec((length, ct), lambda c, s: (s, c))]
        args = [y_prev]
    in_specs += z_specs(zx_col0) + [
        pl.BlockSpec((1, ct), lambda c, s: (0, c)),
        pl.BlockSpec((n, ct), lambda c, s: (0, spec_col0 + c), pipeline_mode=one),
        pl.BlockSpec((n, ct), lambda c, s: (0, spec_col0 + c), pipeline_mode=one),
        pl.BlockSpec(c1.shape, lambda c, s: (0, 0), pipeline_mode=one),
        pl.BlockSpec(g.shape, lambda c, s: (0, 0, 0), pipeline_mode=one),
        pl.BlockSpec(h.shape, lambda c, s: (0, 0, 0), pipeline_mode=one),
        pl.BlockSpec(c2.shape, lambda c, s: (0, 0), pipeline_mode=one),
    ]
    args += [z, conv_w, conv_b, bias_row, spec_r, spec_i, c1, g, h, c2]
    scratch = [
        pltpu.VMEM((length, ct), F32),
        pltpu.VMEM((length, ct), F32),
        pltpu.VMEM((n1 * (2 * n2 + ROW_PAD), ct), F32),
        pltpu.VMEM((n2 * (2 * n1 + ROW_PAD), ct), F32),
    ]
    return pl.pallas_call(
        functools.partial(_hyena_conv_kernel, first=first, n1=n1, n2=n2),
        grid=(d // ct, n_seq), in_specs=in_specs,
        out_specs=pl.BlockSpec((length, ct), lambda c, s: (s, c)),
        out_shape=jax.ShapeDtypeStruct((n_seq * length, d), out_dtype), scratch_shapes=scratch,
        compiler_params=_params(("parallel", "arbitrary")))(*args)


def _hyena_filter_kernel(emb_ref, w1_ref, b1_ref, fr1_ref, w2_ref, b2_ref, fr2_ref, w3_ref, dl_ref, o_ref):
    hi = lax.Precision.HIGHEST
    emb = emb_ref[...]
    hdn = jnp.sin(fr1_ref[...] * (jnp.dot(emb, w1_ref[...], precision=hi, preferred_element_type=F32) + b1_ref[...]))
    hdn = jnp.sin(fr2_ref[...] * (jnp.dot(hdn, w2_ref[...], precision=hi, preferred_element_type=F32) + b2_ref[...]))
    filt = jnp.dot(hdn, w3_ref[...], precision=hi, preferred_element_type=F32)
    t01 = emb[:, 0:1]
    mask = emb[:, LANES - 1:LANES]
    window = jnp.exp(-t01 * dl_ref[...]) + HY_SHIFT
    o_ref[...] = filt * window * mask


def _hyena_spec_kernel(k_ref, c1_ref, g_ref, sr_ref, si_ref, a_scr, *, n1, n2):
    _fft_stage1(k_ref, c1_ref[...], a_scr, n1, n2, n2)

    def mid(k, carry):
        x = jnp.dot(g_ref[k], _fft_stage2_load(a_scr, k, n1, n2), preferred_element_type=F32)
        rows = pl.ds(pl.multiple_of(k * n1, n1), n1)
        sr_ref[rows, :] = x[:n1]
        si_ref[rows, :] = x[n1:]
        return carry

    lax.fori_loop(0, n2, mid, 0, unroll=min(FFT_UNROLL, n2))


def _hyena_spectra(length, d, f_w1, f_b1, f_fr1, f_w2, f_b2, f_fr2, f_w3, mats):
    n = 2 * length
    n1, n2 = _fft_sizes(length)
    lag = jnp.concatenate([jnp.arange(length), jnp.zeros((1,), jnp.int32), jnp.arange(length - 1, 0, -1)])
    t01 = jnp.linspace(0.0, 1.0, length, dtype=F32)[:, None]
    w = 2.0 * math.pi * jnp.arange(length, dtype=F32)[:, None] / length
    f = jnp.linspace(1e-4, HY_BANDS - 1, HY_BANDS, dtype=F32)[None, :]
    emb = jnp.concatenate([t01, jnp.cos(f * w), -jnp.sin(f * w)], -1)[lag]
    mask = jnp.ones((n, 1), F32).at[length, 0].set(0.0)
    emb = jnp.concatenate([emb, jnp.zeros((n, LANES - HY_EMB - 1), F32), mask], -1)

    def pad2(a, r, c):
        return jnp.zeros((r, c), F32).at[:a.shape[0], :a.shape[1]].set(a)

    w1 = pad2(f_w1, LANES, LANES)
    b1 = pad2(f_b1[None], 1, LANES)
    fr1 = pad2(f_fr1[None], 1, LANES)
    w2 = pad2(f_w2, LANES, LANES)
    b2 = pad2(f_b2[None], 1, LANES)
    fr2 = pad2(f_fr2[None], 1, LANES)
    w3 = pad2(f_w3, LANES, f_w3.shape[1])
    deltas = jnp.abs(jnp.linspace(math.log(HY_TARGET) / HY_SLOW_DECAY, math.log(HY_TARGET) / HY_FAST_DECAY,
                                  d, dtype=F32))[None, :]
    tr = min(512, length)
    half_tiles = length // tr
    small = lambda shape: pl.BlockSpec(shape, lambda r, o: (0, 0))
    kern = pl.pallas_call(
        _hyena_filter_kernel, grid=(n // tr, HY_ORDER),
        in_specs=[pl.BlockSpec((tr, LANES), lambda r, o: (r, 0)),
                  small((LANES, LANES)), small((1, LANES)), small((1, LANES)),
                  small((LANES, LANES)), small((1, LANES)), small((1, LANES)),
                  pl.BlockSpec((LANES, d), lambda r, o: (0, 2 * o + r // half_tiles)),
                  small((1, d))],
        out_specs=pl.BlockSpec((tr, d), lambda r, o: (r, o)),
        out_shape=jax.ShapeDtypeStruct((n, HY_ORDER * d), F32),
        compiler_params=_params(("parallel", "parallel")))(emb, w1, b1, fr1, w2, b2, fr2, w3, deltas)
    c1, g, _, _ = mats
    ct = HY_C_TILE
    n_cols = HY_ORDER * d
    one = pl.Buffered(1)
    spec_r, spec_i = pl.pallas_call(
        functools.partial(_hyena_spec_kernel, n1=n1, n2=n2), grid=(n_cols // ct,),
        in_specs=[pl.BlockSpec((n, ct), lambda c: (0, c)),
                  pl.BlockSpec(c1.shape, lambda c: (0, 0), pipeline_mode=one),
                  pl.BlockSpec(g.shape, lambda c: (0, 0, 0), pipeline_mode=one)],
        out_specs=[pl.BlockSpec((n, ct), lambda c: (0, c)), pl.BlockSpec((n, ct), lambda c: (0, c))],
        out_shape=[jax.ShapeDtypeStruct((n, n_cols), F32)] * 2,
        scratch_shapes=[pltpu.VMEM((n1 * (2 * n2 + ROW_PAD), ct), F32)],
        compiler_params=_params(("parallel",)))(kern, c1, g)
    return spec_r, spec_i


def hyena_layer(cfg, xa, mod, w_in, b_in, conv_w, conv_b, f_w1, f_b1, f_fr1, f_w2, f_b2, f_fr2, f_w3,
                bias, w_out, b_out, lng, lnb, ctx_out):
    d = cfg.D
    n_ct = d // HY_C_TILE
    z = fused_linear(xa, w_in.astype(BF16), tm=cfg.tm, tn=3 * d // 2, out_defs=[(3 * d, 3 * d // 2, BF16)],
                     epilogue=lambda acc, rows, cols, m: [acc + cols[0]], seg_fn=_seg_fn(cfg), mod=mod,
                     mod_rows=(0, 1), col_extras=[b_in.reshape(1, -1)],
                     n_rows=None if ctx_out else cfg.NL)[0]
    conv_b2 = conv_b.reshape(1, -1)
    seqs = [(cfg.L, cfg.B, 0)]
    if ctx_out:
        seqs.append((cfg.CL, cfg.B, cfg.NL // cfg.CL))
    ys = []
    for length, n_seq, row_blk0 in seqs:
        mats = _fft_mats(*_fft_sizes(length))
        spec_r, spec_i = _hyena_spectra(length, d, f_w1, f_b1, f_fr1, f_w2, f_b2, f_fr2, f_w3, mats)
        y1 = _hyena_conv(None, z, 2 * n_ct, 0, conv_w, conv_b2, bias[0:1], spec_r, spec_i, 0,
                         mats, length, n_seq, row_blk0, F32)
        ys.append(_hyena_conv(y1, z, 0, n_ct, conv_w, conv_b2, bias[1:2], spec_r, spec_i, n_ct,
                              mats, length, n_seq, row_blk0, BF16))
    y = ys[0] if len(ys) == 1 else jnp.concatenate(ys, 0)
    n_rows = None if ctx_out else cfg.NL
    return residual_ln_linear(cfg, xa, y, w_out.astype(BF16), mod, 2, lng, lnb, bias=b_out, n_rows=n_rows)


def ada_modulation(cfg, cvec, ada_w, ada_b, layer):
    d = cfg.D
    out = fused_linear(cvec, ada_w, w_layer=layer, tm=cvec.shape[0], tn=d, out_defs=[(6 * d, d, F32)],
                       epilogue=lambda acc, rows, cols, m: [acc + cols[0]], pre=jax.nn.silu,
                       col_extras=[ada_b[layer].reshape(1, -1)])[0]
    return out[:cfg.B + 1].reshape(cfg.B + 1, 6, d)


def kernel(x, c, ctx, c_ctx, ada_w, ada_b, ln_g, ln_b, mla_w_dq, mla_q_norm, mla_w_uq, mla_w_dkv, mla_kv_norm,
           mla_w_ukv, mla_w_o, hy_w_in, hy_b_in, hy_conv_w, hy_conv_b, hy_f_w1, hy_f_b1, hy_f_fr1, hy_f_w2,
           hy_f_b2, hy_f_fr2, hy_f_w3, hy_bias, hy_w_out, hy_b_out, gqa_w_qkv, gqa_q_norm, gqa_k_norm, gqa_w_o,
           peer_w_q, peer_keys1, peer_keys2, peer_u, peer_v):
    batch, seq, d = x.shape
    cfg = Cfg(batch, seq, ctx.shape[1], d)
    xa = jnp.concatenate([x.reshape(-1, d), ctx.reshape(-1, d)], 0)
    pad_rows = -(batch + 1) % SUBLANES
    cvec = jnp.concatenate([c, c_ctx[None, :], jnp.zeros((pad_rows, d), F32)], 0)
    peer_w = peer_prepare(peer_w_q, peer_keys1, peer_keys2, peer_u, peer_v)
    ia = ib = ic = 0
    for layer in range(DEPTH):
        last = layer == DEPTH - 1
        mod = ada_modulation(cfg, cvec, ada_w, ada_b, layer)
        lng, lnb = ln_g[layer, 0], ln_b[layer, 0]
        kind = layer % N_MIXERS
        if kind == 0:
            xm = mla_layer(cfg, xa, mod, mla_w_dq[ia], mla_q_norm[ia], mla_w_uq[ia], mla_w_dkv[ia],
                           mla_kv_norm[ia], mla_w_ukv[ia], mla_w_o[ia], lng, lnb, not last)
            ia += 1
        elif kind == 1:
            xm = hyena_layer(cfg, xa, mod, hy_w_in[ib], hy_b_in[ib], hy_conv_w[ib], hy_conv_b[ib], hy_f_w1[ib],
                             hy_f_b1[ib], hy_f_fr1[ib], hy_f_w2[ib], hy_f_b2[ib], hy_f_fr2[ib], hy_f_w3[ib],
                             hy_bias[ib], hy_w_out[ib], hy_b_out[ib], lng, lnb, not last)
            ib += 1
        else:
            xm = gqa_layer(cfg, xa, mod, gqa_w_qkv[ic], gqa_q_norm[ic], gqa_k_norm[ic], gqa_w_o[ic],
                           lng, lnb, not last)
            ic += 1
        xa = peer_layer(cfg, xm, mod, layer, *peer_w, ln_g[layer, 1], ln_b[layer, 1], xm.shape[0])
    return xa[:cfg.NL].reshape(batch, seq, d)
```
